```python
import jax, jax.numpy as jnp
from jax import lax
import numpy as np

D_MODEL = 4096
BATCH = 4
SEQ = 2048
DEPTH = 2

CHUNK = 64
PLE_DIM = 256
D_FF = 11008
HEAD_DIM = 128
N_HEADS_FOX = D_MODEL // (2 * HEAD_DIM)
N_HEADS_CHUNK = D_MODEL // (2 * HEAD_DIM)
WIDTH_FOX = N_HEADS_FOX * HEAD_DIM
WIDTH_CHUNK = N_HEADS_CHUNK * HEAD_DIM
LEFT_CHUNKS = 8
BAND_CHUNKS = LEFT_CHUNKS + 1
REL_CLIP = 128
Q_BLOCK = 128
RMS_EPS = 1e-6
IN_SPLITS = [WIDTH_FOX, WIDTH_FOX, WIDTH_FOX, WIDTH_CHUNK, WIDTH_CHUNK, WIDTH_CHUNK, N_HEADS_FOX]
N_IN = sum(IN_SPLITS)

kernel_name = 'hybrid_fox_chunkrel_macaron_ple'


def rms_norm(x, g):
    x32 = x.astype(jnp.float32)
    y = x32 * lax.rsqrt(jnp.mean(x32 * x32, axis=-1, keepdims=True) + RMS_EPS)
    return (y * g.astype(jnp.float32)).astype(x.dtype)


def swiglu(u, w_gate, w_up, w_down):
    return (jax.nn.silu(u @ w_gate) * (u @ w_up)) @ w_down


def split_heads(t, n_heads):
    b, s, _ = t.shape
    return t.reshape(b, s, n_heads, HEAD_DIM).transpose(0, 2, 1, 3)


def forgetting_attention(q, k, v, log_f):
    b, h, s, dh = q.shape
    nb = s // Q_BLOCK
    qf = q.astype(jnp.float32) * (dh ** -0.5)
    kf = k.astype(jnp.float32)
    vf = v.astype(jnp.float32)
    c = jnp.cumsum(log_f, axis=-1)
    q_blocks = qf.reshape(b, h, nb, Q_BLOCK, dh).transpose(2, 0, 1, 3, 4)
    c_blocks = c.reshape(b, h, nb, Q_BLOCK).transpose(2, 0, 1, 3)
    key_pos = jnp.arange(s)

    def one_block(args):
        qb, cb, i = args
        q_pos = i * Q_BLOCK + jnp.arange(Q_BLOCK)
        sc = jnp.einsum('bhqd,bhkd->bhqk', qb, kf) + cb[..., :, None] - c[:, :, None, :]
        sc = jnp.where(q_pos[:, None] >= key_pos[None, :], sc, -jnp.inf)
        return jnp.einsum('bhqk,bhkd->bhqd', jax.nn.softmax(sc, axis=-1), vf)

    o = lax.map(one_block, (q_blocks, c_blocks, jnp.arange(nb)))
    return o.transpose(1, 0, 3, 2, 4).reshape(b, s, h * dh)


def chunked_rel_attention(q, k, v, rel_bias):
    b, h, s, dh = q.shape
    nc = s // CHUNK
    qc = (q.astype(jnp.float32) * (dh ** -0.5)).reshape(b, h, nc, CHUNK, dh)
    pad = ((0, 0), (0, 0), (LEFT_CHUNKS, 0), (0, 0), (0, 0))
    kc = jnp.pad(k.astype(jnp.float32).reshape(b, h, nc, CHUNK, dh), pad)
    vc = jnp.pad(v.astype(jnp.float32).reshape(b, h, nc, CHUNK, dh), pad)
    sc = jnp.concatenate(
        [jnp.einsum('bhcqd,bhckd->bhcqk', qc, kc[:, :, j:j + nc]) for j in range(BAND_CHUNKS)],
        axis=-1)
    q_off = jnp.arange(CHUNK)
    band_pos = jnp.arange(BAND_CHUNKS * CHUNK) - LEFT_CHUNKS * CHUNK
    dist = q_off[:, None] - band_pos[None, :]
    idx = jnp.clip(dist, -REL_CLIP, REL_CLIP) + REL_CLIP
    bias = rel_bias.astype(jnp.float32)[:, idx]
    src_chunk = jnp.arange(nc)[:, None] + band_pos[None, :] // CHUNK
    valid = src_chunk >= 0
    sc = jnp.where(valid[None, None, :, None, :], sc + bias[None, :, None], -jnp.inf)
    pr = jax.nn.softmax(sc, axis=-1)
    o = jnp.einsum('bhcqk,bhckd->bhcqd', pr[..., :CHUNK], vc[:, :, 0:nc])
    for j in range(1, BAND_CHUNKS):
        o = o + jnp.einsum('bhcqk,bhckd->bhcqd', pr[..., j * CHUNK:(j + 1) * CHUNK], vc[:, :, j:j + nc])
    return o.reshape(b, h, s, dh).transpose(0, 2, 1, 3).reshape(b, s, h * dh)


def setup_inputs(seed: int = 0) -> dict:
    key = jax.random.key(seed)
    ks = jax.random.split(key, 24)
    nrm = lambda k, shape, fan_in: jax.random.normal(k, shape, jnp.float32) * (fan_in ** -0.5)
    gain = lambda k, shape: 1.0 + 0.01 * jax.random.normal(k, shape, jnp.float32)
    return {
        'x': jax.random.normal(ks[0], (BATCH, SEQ, D_MODEL), jnp.float32),
        'p': jax.random.normal(ks[1], (DEPTH, BATCH, SEQ, PLE_DIM), jnp.float32),
        'ffn1_norm': gain(ks[2], (DEPTH, D_MODEL)),
        'ffn1_w_gate': nrm(ks[3], (DEPTH, D_MODEL, D_FF), D_MODEL),
        'ffn1_w_up': nrm(ks[4], (DEPTH, D_MODEL, D_FF), D_MODEL),
        'ffn1_w_down': nrm(ks[5], (DEPTH, D_FF, D_MODEL), D_FF),
        'mix_norm': gain(ks[6], (DEPTH, D_MODEL)),
        'w_in': nrm(ks[7], (DEPTH, D_MODEL, N_IN), D_MODEL),
        'fox_forget_bias': 3.0 + 0.1 * jax.random.normal(ks[8], (DEPTH, N_HEADS_FOX), jnp.float32),
        'rel_bias': 0.5 * jax.random.normal(ks[9], (DEPTH, N_HEADS_CHUNK, 2 * REL_CLIP + 1), jnp.float32),
        'w_branch_gate': nrm(ks[10], (DEPTH, D_MODEL, 2 * D_MODEL), D_MODEL),
        'w_proj_a': nrm(ks[11], (DEPTH, WIDTH_FOX, D_MODEL), WIDTH_FOX),
        'w_proj_b': nrm(ks[12], (DEPTH, WIDTH_CHUNK, D_MODEL), WIDTH_CHUNK),
        'w_out': nrm(ks[13], (DEPTH, D_MODEL, D_MODEL), D_MODEL),
        'ffn2_norm': gain(ks[14], (DEPTH, D_MODEL)),
        'ffn2_w_gate': nrm(ks[15], (DEPTH, D_MODEL, D_FF), D_MODEL),
        'ffn2_w_up': nrm(ks[16], (DEPTH, D_MODEL, D_FF), D_MODEL),
        'ffn2_w_down': nrm(ks[17], (DEPTH, D_FF, D_MODEL), D_FF),
        'ple_norm': gain(ks[18], (DEPTH, D_MODEL)),
        'ple_w_gate': nrm(ks[19], (DEPTH, D_MODEL, D_MODEL), D_MODEL),
        'ple_w_proj': nrm(ks[20], (DEPTH, PLE_DIM, D_MODEL), PLE_DIM),
        'final_norm': gain(ks[21], (D_MODEL,)),
    }


def reference(x, p, ffn1_norm, ffn1_w_gate, ffn1_w_up, ffn1_w_down, mix_norm, w_in,
              fox_forget_bias, rel_bias, w_branch_gate, w_proj_a, w_proj_b, w_out,
              ffn2_norm, ffn2_w_gate, ffn2_w_up, ffn2_w_down, ple_norm, ple_w_gate,
              ple_w_proj, final_norm):
    split_at = [int(v) for v in np.cumsum(IN_SPLITS)[:-1]]
    h = x
    for i in range(DEPTH):
        h = h + 0.5 * swiglu(rms_norm(h, ffn1_norm[i]), ffn1_w_gate[i], ffn1_w_up[i], ffn1_w_down[i])
        u = rms_norm(h, mix_norm[i])
        z = u @ w_in[i]
        q_a, k_a, v_a, q_b, k_b, v_b, f_logit = jnp.split(z, split_at, axis=-1)
        log_f = jax.nn.log_sigmoid((f_logit + fox_forget_bias[i]).astype(jnp.float32)).transpose(0, 2, 1)
        y_a = forgetting_attention(split_heads(q_a, N_HEADS_FOX), split_heads(k_a, N_HEADS_FOX),
                                   split_heads(v_a, N_HEADS_FOX), log_f).astype(h.dtype) @ w_proj_a[i]
        y_b = chunked_rel_attention(split_heads(q_b, N_HEADS_CHUNK), split_heads(k_b, N_HEADS_CHUNK),
                                    split_heads(v_b, N_HEADS_CHUNK), rel_bias[i]).astype(h.dtype) @ w_proj_b[i]
        g_a, g_b = jnp.split(jax.nn.sigmoid(u @ w_branch_gate[i]), 2, axis=-1)
        h = h + (g_a * y_a + g_b * y_b) @ w_out[i]
        h = h + 0.5 * swiglu(rms_norm(h, ffn2_norm[i]), ffn2_w_gate[i], ffn2_w_up[i], ffn2_w_down[i])
        ple_gate = jax.nn.sigmoid(rms_norm(h, ple_norm[i]) @ ple_w_gate[i])
        h = h + ple_gate * (p[i].astype(h.dtype) @ ple_w_proj[i])
    return rms_norm(h, final_norm)
```

```python
import functools

import jax
import jax.numpy as jnp
import numpy as np
from jax import lax
from jax.experimental import pallas as pl
from jax.experimental.pallas import tpu as pltpu

D_MODEL = 4096
BATCH = 4
SEQ = 2048
DEPTH = 2
CHUNK = 64
PLE_DIM = 256
D_FF = 11008
HEAD_DIM = 128
N_HEADS = D_MODEL // (2 * HEAD_DIM)
WIDTH = N_HEADS * HEAD_DIM
LEFT_CHUNKS = 8
REL_CLIP = 128
RMS_EPS = 1e-6
N_QKV = 6 * WIDTH
M_TOK = BATCH * SEQ

F32 = jnp.float32
BF16 = jnp.bfloat16

LANES = 128
V7X_VMEM_BYTES = 64 * 1024 * 1024
V7X_VMEM_REQUEST_CAP = 60000 * 1024
MIB = 1024 * 1024


def _nbytes(shape, dtype):
    return int(np.prod(shape)) * jnp.dtype(dtype).itemsize


def _vmem_limit(blocks, scratch=(), temps=()):
    total = 2 * sum(_nbytes(s, d) for s, d in blocks)
    total += sum(_nbytes(s, d) for s, d in scratch)
    total += sum(_nbytes(s, d) for s, d in temps)
    total += 4 * MIB
    return min(total, V7X_VMEM_REQUEST_CAP)


def _params(semantics, limit):
    return pltpu.CompilerParams(dimension_semantics=semantics, vmem_limit_bytes=limit)


def _rmsnorm_kernel(h_ref, g_ref, o_ref):
    x = h_ref[...]
    ms = jnp.mean(x * x, axis=-1, keepdims=True)
    o_ref[...] = (x * lax.rsqrt(ms + RMS_EPS) * g_ref[...]).astype(o_ref.dtype)


def _rmsnorm(h, g, out_dtype, tr=256):
    m, d = h.shape
    blocks = [((tr, d), F32), ((1, d), F32), ((tr, d), out_dtype)]
    return pl.pallas_call(
        _rmsnorm_kernel,
        grid=(m // tr,),
        in_specs=[pl.BlockSpec((tr, d), lambda i: (i, 0)),
                  pl.BlockSpec((1, d), lambda i: (0, 0))],
        out_specs=pl.BlockSpec((tr, d), lambda i: (i, 0)),
        out_shape=jax.ShapeDtypeStruct((m, d), out_dtype),
        compiler_params=_params(("parallel",), _vmem_limit(blocks, temps=[((tr, d), F32)] * 2)),
        name="rmsnorm",
    )(h, g.reshape(1, d))


def _gateup_kernel(n_ref, wg_ref, wu_ref, o_ref):
    n = n_ref[...]
    g = jnp.dot(n, wg_ref[...], preferred_element_type=F32)
    u = jnp.dot(n, wu_ref[...], preferred_element_type=F32)
    o_ref[...] = (g * jax.nn.sigmoid(g) * u).astype(o_ref.dtype)


def _gateup(n, wg, wu, tm=2048, tn=256):
    m, d = n.shape
    f = wg.shape[1]
    blocks = [((tm, d), BF16), ((d, tn), BF16), ((d, tn), BF16), ((tm, tn), BF16)]
    return pl.pallas_call(
        _gateup_kernel,
        grid=(m // tm, f // tn),
        in_specs=[pl.BlockSpec((tm, d), lambda i, j: (i, 0)),
                  pl.BlockSpec((d, tn), lambda i, j: (0, j)),
                  pl.BlockSpec((d, tn), lambda i, j: (0, j))],
        out_specs=pl.BlockSpec((tm, tn), lambda i, j: (i, j)),
        out_shape=jax.ShapeDtypeStruct((m, f), BF16),
        compiler_params=_params(("parallel", "parallel"),
                                _vmem_limit(blocks, temps=[((tm, tn), F32)] * 4)),
        name="ffn_gateup",
    )(n, wg, wu)


def _down_kernel(a_ref, w_ref, r_ref, o_ref):
    acc = jnp.dot(a_ref[...], w_ref[...], preferred_element_type=F32)
    o_ref[...] = r_ref[...] + 0.5 * acc


def _down(act, wd, h, tm=512, tn=512):
    m, f = act.shape
    d = wd.shape[1]
    blocks = [((tm, f), BF16), ((f, tn), BF16), ((tm, tn), F32), ((tm, tn), F32)]
    return pl.pallas_call(
        _down_kernel,
        grid=(m // tm, d // tn),
        in_specs=[pl.BlockSpec((tm, f), lambda i, j: (i, 0)),
                  pl.BlockSpec((f, tn), lambda i, j: (0, j)),
                  pl.BlockSpec((tm, tn), lambda i, j: (i, j))],
        out_specs=pl.BlockSpec((tm, tn), lambda i, j: (i, j)),
        out_shape=jax.ShapeDtypeStruct((m, d), F32),
        compiler_params=_params(("parallel", "parallel"),
                                _vmem_limit(blocks, temps=[((tm, tn), F32)] * 2)),
        name="ffn_down",
    )(act, wd, h)


def _proj_kernel(a_ref, w_ref, o_ref):
    o_ref[...] = jnp.dot(a_ref[...], w_ref[...], preferred_element_type=F32).astype(o_ref.dtype)


def _qkv_proj(u, w, tm=1024, tn=1024):
    m, d = u.shape
    n = w.shape[1]
    blocks = [((tm, d), BF16), ((d, tn), BF16), ((tm, tn), BF16)]
    return pl.pallas_call(
        _proj_kernel,
        grid=(m // tm, n // tn),
        in_specs=[pl.BlockSpec((tm, d), lambda i, j: (i, 0)),
                  pl.BlockSpec((d, tn), lambda i, j: (0, j))],
        out_specs=pl.BlockSpec((tm, tn), lambda i, j: (i, j)),
        out_shape=jax.ShapeDtypeStruct((m, n), BF16),
        compiler_params=_params(("parallel", "parallel"),
                                _vmem_limit(blocks, temps=[((tm, tn), F32)] * 2)),
        name="qkv_proj",
    )(u, w)


def _log_sigmoid(x):
    return jnp.minimum(x, 0.0) - jnp.log1p(jnp.exp(-jnp.abs(x)))


def _forget_kernel(u_ref, w_ref, b_ref, c_ref, carry_ref, *, ts):
    @pl.when(pl.program_id(1) == 0)
    def _():
        carry_ref[...] = jnp.zeros_like(carry_ref)

    logit = jnp.dot(u_ref[0], w_ref[...], preferred_element_type=F32) + b_ref[...]
    c = _log_sigmoid(logit)
    row = lax.broadcasted_iota(jnp.int32, c.shape, 0)
    shift = 1
    while shift < ts:
        c = c + jnp.where(row >= shift, pltpu.roll(c, shift, axis=0), 0.0)
        shift *= 2
    c = c + carry_ref[...]
    c_ref[0] = c
    carry_ref[...] = c[ts - 1:ts, :]


def _forget_cumsum(u3, wf, bias, ts=512):
    b, s, d = u3.shape
    blocks = [((1, ts, d), BF16), ((d, LANES), BF16), ((1, LANES), F32), ((1, ts, LANES), F32)]
    return pl.pallas_call(
        functools.partial(_forget_kernel, ts=ts),
        grid=(b, s // ts),
        in_specs=[pl.BlockSpec((1, ts, d), lambda bi, si: (bi, si, 0)),
                  pl.BlockSpec((d, LANES), lambda bi, si: (0, 0)),
                  pl.BlockSpec((1, LANES), lambda bi, si: (0, 0))],
        out_specs=pl.BlockSpec((1, ts, LANES), lambda bi, si: (bi, si, 0)),
        out_shape=jax.ShapeDtypeStruct((b, s, LANES), F32),
        scratch_shapes=[pltpu.VMEM((1, LANES), F32)],
        compiler_params=_params(("parallel", "arbitrary"),
                                _vmem_limit(blocks, temps=[((ts, LANES), F32)] * 8)),
        name="forget_cumsum",
    )(u3, wf, bias)


def _fox_kernel(q_ref, k_ref, v_ref, c_ref, ct_ref, o_ref, *, tq):
    h = pl.program_id(1)
    i = pl.program_id(2)
    q = (q_ref[0].astype(F32) * (HEAD_DIM ** -0.5)).astype(BF16)
    lane = lax.broadcasted_iota(jnp.int32, (tq, LANES), 1)
    c_t = jnp.sum(jnp.where(lane == h, c_ref[0], 0.0), axis=1, keepdims=True)

    def block(j, carry, masked):
        m, l, acc = carry
        start = pl.multiple_of(j * tq, tq)
        k = k_ref[0, pl.ds(start, tq), :]
        v = v_ref[0, pl.ds(start, tq), :]
        s = lax.dot_general(q, k, (((1,), (1,)), ((), ())), preferred_element_type=F32)
        s = s + (c_t - ct_ref[0, 0, :, pl.ds(start, tq)])
        if masked:
            r = lax.broadcasted_iota(jnp.int32, (tq, tq), 0)
            col = lax.broadcasted_iota(jnp.int32, (tq, tq), 1)
            s = jnp.where(r >= col, s, -jnp.inf)
        m_new = jnp.maximum(m, jnp.max(s, axis=1, keepdims=True))
        alpha = jnp.exp(m - m_new)
        p = jnp.exp(s - m_new)
        l = alpha * l + jnp.sum(p, axis=1, keepdims=True)
        acc = alpha * acc + jnp.dot(p.astype(BF16), v, preferred_element_type=F32)
        return m_new, l, acc

    init = (jnp.full((tq, 1), -jnp.inf, F32), jnp.zeros((tq, 1), F32),
            jnp.zeros((tq, HEAD_DIM), F32))
    carry = lax.fori_loop(0, i, lambda j, c: block(j, c, False), init)
    _, l, acc = block(i, carry, True)
    o_ref[0] = (acc / l).astype(o_ref.dtype)


def _fox_attention(z3, c, ct, tq=512):
    b, s, _ = z3.shape
    blocks = [((1, tq, HEAD_DIM), BF16), ((1, s, HEAD_DIM), BF16), ((1, s, HEAD_DIM), BF16),
              ((1, tq, LANES), F32), ((1, 1, 1, s), F32), ((1, tq, HEAD_DIM), BF16)]
    return pl.pallas_call(
        functools.partial(_fox_kernel, tq=tq),
        grid=(b, N_HEADS, s // tq),
        in_specs=[pl.BlockSpec((1, tq, HEAD_DIM), lambda bi, hi, i: (bi, i, hi)),
                  pl.BlockSpec((1, s, HEAD_DIM), lambda bi, hi, i: (bi, 0, N_HEADS + hi)),
                  pl.BlockSpec((1, s, HEAD_DIM), lambda bi, hi, i: (bi, 0, 2 * N_HEADS + hi)),
                  pl.BlockSpec((1, tq, LANES), lambda bi, hi, i: (bi, i, 0)),
                  pl.BlockSpec((1, 1, 1, s), lambda bi, hi, i: (bi, hi, 0, 0))],
        out_specs=pl.BlockSpec((1, tq, HEAD_DIM), lambda bi, hi, i: (bi, i, hi)),
        out_shape=jax.ShapeDtypeStruct((b, s, WIDTH), BF16),
        compiler_params=_params(("parallel", "parallel", "arbitrary"),
                                _vmem_limit(blocks, temps=[((tq, tq), F32)] * 6)),
        name="fox_attention",
    )(z3, z3, z3, c, ct)


_QB = 8 * CHUNK
_N_BIAS_VEC = 12


def _bias_pieces_index():
    e = 128 * (np.arange(_N_BIAS_VEC)[:, None] - 4) + np.arange(LANES)[None, :]
    dist = _QB - e
    return np.clip(dist, -REL_CLIP, REL_CLIP) + REL_CLIP


def _chunk_kernel(q_ref, kp_ref, kc_ref, vp_ref, vc_ref, g_ref, o_ref, bias_ref):
    bi = pl.program_id(1)
    i = pl.program_id(2)

    @pl.when((bi == 0) & (i == 0))
    def _():
        r = lax.broadcasted_iota(jnp.int32, (LANES, LANES), 0)
        col = lax.broadcasted_iota(jnp.int32, (LANES, LANES), 1)
        upper = col >= r
        rolled = [pltpu.roll(jnp.broadcast_to(g_ref[0, k:k + 1, :], (LANES, LANES)), 0, 1,
                             stride=1, stride_axis=0) for k in range(_N_BIAS_VEC)]
        for rb in range(_QB // LANES):
            for cb in range(2 * _QB // LANES):
                delta = cb - rb + 4
                tile = jnp.where(upper, rolled[delta], rolled[delta - 1])
                q_chunk = (rb * LANES + r) // CHUNK
                k_chunk = (cb * LANES + col) // CHUNK
                ok = (k_chunk >= q_chunk) & (k_chunk <= q_chunk + LEFT_CHUNKS)
                bias_ref[rb * LANES:(rb + 1) * LANES, cb * LANES:(cb + 1) * LANES] = (
                    jnp.where(ok, tile, -jnp.inf))

    q = (q_ref[0].astype(F32) * (HEAD_DIM ** -0.5)).astype(BF16)
    dn = (((1,), (1,)), ((), ()))
    s_prev = lax.dot_general(q, kp_ref[0], dn, preferred_element_type=F32) + bias_ref[:, :_QB]
    s_prev = jnp.where(i > 0, s_prev, -jnp.inf)
    s_cur = lax.dot_general(q, kc_ref[0], dn, preferred_element_type=F32) + bias_ref[:, _QB:]
    m = jnp.maximum(jnp.max(s_prev, axis=1, keepdims=True), jnp.max(s_cur, axis=1, keepdims=True))
    p_prev = jnp.exp(s_prev - m)
    p_cur = jnp.exp(s_cur - m)
    l = jnp.sum(p_prev, axis=1, keepdims=True) + jnp.sum(p_cur, axis=1, keepdims=True)
    acc = jnp.dot(p_prev.astype(BF16), vp_ref[0], preferred_element_type=F32)
    acc = acc + jnp.dot(p_cur.astype(BF16), vc_ref[0], preferred_element_type=F32)
    o_ref[0] = (acc / l).astype(o_ref.dtype)


def _chunk_attention(z3, bias_pieces):
    b, s, _ = z3.shape
    nb = s // _QB
    qo, ko, vo = 3 * N_HEADS, 4 * N_HEADS, 5 * N_HEADS
    blk = (1, _QB, HEAD_DIM)
    blocks = [(blk, BF16)] * 6 + [((1, 16, LANES), F32)]
    prev = lambda i: jnp.maximum(i - 1, 0)
    return pl.pallas_call(
        _chunk_kernel,
        grid=(N_HEADS, b, nb),
        in_specs=[pl.BlockSpec(blk, lambda hi, bi, i: (bi, i, qo + hi)),
                  pl.BlockSpec(blk, lambda hi, bi, i: (bi, prev(i), ko + hi)),
                  pl.BlockSpec(blk, lambda hi, bi, i: (bi, i, ko + hi)),
                  pl.BlockSpec(blk, lambda hi, bi, i: (bi, prev(i), vo + hi)),
                  pl.BlockSpec(blk, lambda hi, bi, i: (bi, i, vo + hi)),
                  pl.BlockSpec((1, 16, LANES), lambda hi, bi, i: (hi, 0, 0))],
        out_specs=pl.BlockSpec(blk, lambda hi, bi, i: (bi, i, hi)),
        out_shape=jax.ShapeDtypeStruct((b, s, WIDTH), BF16),
        scratch_shapes=[pltpu.VMEM((_QB, 2 * _QB), F32)],
        compiler_params=_params(("arbitrary", "arbitrary", "arbitrary"),
                                _vmem_limit(blocks, scratch=[((_QB, 2 * _QB), F32)],
                                            temps=[((_QB, _QB), F32)] * 8)),
        name="chunk_attention",
    )(z3, z3, z3, z3, z3, bias_pieces)


def _branch_kernel(u_ref, aa_ref, ab_ref, wga_ref, wgb_ref, wpa_ref, wpb_ref, o_ref):
    u = u_ref[...]
    g_a = jax.nn.sigmoid(jnp.dot(u, wga_ref[...], preferred_element_type=F32))
    g_b = jax.nn.sigmoid(jnp.dot(u, wgb_ref[...], preferred_element_type=F32))
    y_a = jnp.dot(aa_ref[...], wpa_ref[...], preferred_element_type=F32)
    y_b = jnp.dot(ab_ref[...], wpb_ref[...], preferred_element_type=F32)
    o_ref[...] = (g_a * y_a + g_b * y_b).astype(o_ref.dtype)


def _branch_mix(u, aa, ab, w_gate, w_pa, w_pb, tm=1024, tn=256):
    m, d = u.shape
    w = aa.shape[1]
    nj = d // tn
    blocks = [((tm, d), BF16), ((tm, w), BF16), ((tm, w), BF16), ((d, tn), BF16), ((d, tn), BF16),
              ((w, tn), BF16), ((w, tn), BF16), ((tm, tn), BF16)]
    return pl.pallas_call(
        _branch_kernel,
        grid=(m // tm, nj),
        in_specs=[pl.BlockSpec((tm, d), lambda i, j: (i, 0)),
                  pl.BlockSpec((tm, w), lambda i, j: (i, 0)),
                  pl.BlockSpec((tm, w), lambda i, j: (i, 0)),
                  pl.BlockSpec((d, tn), lambda i, j: (0, j)),
                  pl.BlockSpec((d, tn), lambda i, j: (0, nj + j)),
                  pl.BlockSpec((w, tn), lambda i, j: (0, j)),
                  pl.BlockSpec((w, tn), lambda i, j: (0, j))],
        out_specs=pl.BlockSpec((tm, tn), lambda i, j: (i, j)),
        out_shape=jax.ShapeDtypeStruct((m, d), BF16),
        compiler_params=_params(("parallel", "parallel"),
                                _vmem_limit(blocks, temps=[((tm, tn), F32)] * 6)),
        name="branch_mix",
    )(u, aa, ab, w_gate, w_gate, w_pa, w_pb)


def _resproj_kernel(a_ref, w_ref, r_ref, o_ref):
    o_ref[...] = r_ref[...] + jnp.dot(a_ref[...], w_ref[...], preferred_element_type=F32)


def _residual_proj(a, w, h, tm=1024, tn=512):
    m, k = a.shape
    d = w.shape[1]
    blocks = [((tm, k), BF16), ((k, tn), BF16), ((tm, tn), F32), ((tm, tn), F32)]
    return pl.pallas_call(
        _resproj_kernel,
        grid=(m // tm, d // tn),
        in_specs=[pl.BlockSpec((tm, k), lambda i, j: (i, 0)),
                  pl.BlockSpec((k, tn), lambda i, j: (0, j)),
                  pl.BlockSpec((tm, tn), lambda i, j: (i, j))],
        out_specs=pl.BlockSpec((tm, tn), lambda i, j: (i, j)),
        out_shape=jax.ShapeDtypeStruct((m, d), F32),
        compiler_params=_params(("parallel", "parallel"),
                                _vmem_limit(blocks, temps=[((tm, tn), F32)] * 2)),
        name="out_proj",
    )(a, w, h)


def _ple_kernel(n_ref, p_ref, wg_ref, wp_ref, r_ref, o_ref):
    gate = jax.nn.sigmoid(jnp.dot(n_ref[...], wg_ref[...], preferred_element_type=F32))
    emb = jnp.dot(p_ref[...], wp_ref[...], preferred_element_type=F32)
    o_ref[...] = r_ref[...] + gate * emb


def _ple(n, p, w_gate, w_proj, h, tm=1024, tn=512):
    m, d = n.shape
    kp = p.shape[1]
    blocks = [((tm, d), BF16), ((tm, kp), BF16), ((d, tn), BF16), ((kp, tn), BF16),
              ((tm, tn), F32), ((tm, tn), F32)]
    return pl.pallas_call(
        _ple_kernel,
        grid=(m // tm, d // tn),
        in_specs=[pl.BlockSpec((tm, d), lambda i, j: (i, 0)),
                  pl.BlockSpec((tm, kp), lambda i, j: (i, 0)),
                  pl.BlockSpec((d, tn), lambda i, j: (0, j)),
                  pl.BlockSpec((kp, tn), lambda i, j: (0, j)),
                  pl.BlockSpec((tm, tn), lambda i, j: (i, j))],
        out_specs=pl.BlockSpec((tm, tn), lambda i, j: (i, j)),
        out_shape=jax.ShapeDtypeStruct((m, d), F32),
        compiler_params=_params(("parallel", "parallel"),
                                _vmem_limit(blocks, temps=[((tm, tn), F32)] * 4)),
        name="ple",
    )(n, p, w_gate, w_proj, h)


def kernel(x, p, ffn1_norm, ffn1_w_gate, ffn1_w_up, ffn1_w_down, mix_norm, w_in, fox_forget_bias,
           rel_bias, w_branch_gate, w_proj_a, w_proj_b, w_out, ffn2_norm, ffn2_w_gate, ffn2_w_up,
           ffn2_w_down, ple_norm, ple_w_gate, ple_w_proj, final_norm):
    bf = lambda w: w.astype(BF16)
    ffn1_wg, ffn1_wu, ffn1_wd = bf(ffn1_w_gate), bf(ffn1_w_up), bf(ffn1_w_down)
    ffn2_wg, ffn2_wu, ffn2_wd = bf(ffn2_w_gate), bf(ffn2_w_up), bf(ffn2_w_down)
    w_qkv = bf(w_in[:, :, :N_QKV])
    w_forget = bf(jnp.pad(w_in[:, :, N_QKV:], ((0, 0), (0, 0), (0, LANES - N_HEADS))))
    forget_bias = jnp.pad(fox_forget_bias, ((0, 0), (0, LANES - N_HEADS)))[:, None, :]
    w_bgate, w_pa, w_pb, w_o = bf(w_branch_gate), bf(w_proj_a), bf(w_proj_b), bf(w_out)
    w_pgate, w_pproj = bf(ple_w_gate), bf(ple_w_proj)
    p_bf = bf(p).reshape(DEPTH, M_TOK, PLE_DIM)
    bias_pieces = jnp.pad(rel_bias[:, :, _bias_pieces_index()],
                          ((0, 0), (0, 0), (0, 16 - _N_BIAS_VEC), (0, 0)))

    h = x.reshape(M_TOK, D_MODEL)
    for i in range(DEPTH):
        n = _rmsnorm(h, ffn1_norm[i], BF16)
        h = _down(_gateup(n, ffn1_wg[i], ffn1_wu[i]), ffn1_wd[i], h)

        u = _rmsnorm(h, mix_norm[i], BF16)
        z3 = _qkv_proj(u, w_qkv[i]).reshape(BATCH, SEQ, N_QKV)
        c = _forget_cumsum(u.reshape(BATCH, SEQ, D_MODEL), w_forget[i], forget_bias[i])
        ct = c[:, :, :N_HEADS].transpose(0, 2, 1)[:, :, None, :]
        attn_a = _fox_attention(z3, c, ct).reshape(M_TOK, WIDTH)
        attn_b = _chunk_attention(z3, bias_pieces[i]).reshape(M_TOK, WIDTH)
        mix = _branch_mix(u, attn_a, attn_b, w_bgate[i], w_pa[i], w_pb[i])
        h = _residual_proj(mix, w_o[i], h)

        n = _rmsnorm(h, ffn2_norm[i], BF16)
        h = _down(_gateup(n, ffn2_wg[i], ffn2_wu[i]), ffn2_wd[i], h)

        n = _rmsnorm(h, ple_norm[i], BF16)
        h = _ple(n, p_bf[i], w_pgate[i], w_pproj[i], h)
    out = _rmsnorm(h, final_norm, F32)
    return out.reshape(BATCH, SEQ, D_MODEL)
```

```python
import functools
import math

import jax
import jax.numpy as jnp
import numpy as np
from jax import lax
from jax.experimental import pallas as pl
from jax.experimental.pallas import tpu as pltpu

D_MODEL = 4096
BATCH = 4
SEQ = 2048
DEPTH = 2
CHUNK = 64
PLE_DIM = 256
D_FF = 11008
HEAD_DIM = 128
N_HEADS = D_MODEL // (2 * HEAD_DIM)
WIDTH = N_HEADS * HEAD_DIM
LEFT_CHUNKS = 8
REL_CLIP = 128
RMS_EPS = 1e-6
N_QKV = 6 * WIDTH
M_TOK = BATCH * SEQ
LOG2E = math.log2(math.e)
Q_SCALE = HEAD_DIM ** -0.5 * LOG2E

F32 = jnp.float32
BF16 = jnp.bfloat16

LANES = 128
V7X_VMEM_REQUEST_CAP = 60000 * 1024
MIB = 1024 * 1024

_NT = (((1,), (1,)), ((), ()))
_TN = (((0,), (0,)), ((), ()))


def _nbytes(shape, dtype):
    return int(np.prod([s for s in shape if s is not None])) * jnp.dtype(dtype).itemsize


def _vmem_limit(blocks, single=(), temps=()):
    total = 2 * sum(_nbytes(s, d) for s, d in blocks)
    total += sum(_nbytes(s, d) for s, d in single)
    total += sum(_nbytes(s, d) for s, d in temps)
    total += 4 * MIB
    return min(total, V7X_VMEM_REQUEST_CAP)


def _params(semantics, limit):
    return pltpu.CompilerParams(dimension_semantics=semantics, vmem_limit_bytes=limit)


def _rmsnorm_kernel(h_ref, g_ref, o_ref):
    x = h_ref[...]
    ms = jnp.mean(x * x, axis=-1, keepdims=True)
    o_ref[...] = (x * lax.rsqrt(ms + RMS_EPS) * g_ref[...]).astype(o_ref.dtype)


def _rmsnorm(h, g, out_dtype, tr=512):
    m, d = h.shape
    blocks = [((tr, d), F32), ((1, d), F32), ((tr, d), out_dtype)]
    return pl.pallas_call(
        _rmsnorm_kernel,
        grid=(m // tr,),
        in_specs=[pl.BlockSpec((tr, d), lambda i: (i, 0)),
                  pl.BlockSpec((1, d), lambda i: (0, 0))],
        out_specs=pl.BlockSpec((tr, d), lambda i: (i, 0)),
        out_shape=jax.ShapeDtypeStruct((m, d), out_dtype),
        compiler_params=_params(("parallel",), _vmem_limit(blocks, temps=[((tr, d), F32)] * 2)),
        name="rmsnorm",
    )(h, g.reshape(1, d))


def _linear_kernel(*refs, w_act, n_acts, n_extras, epilogue, cast):
    nw = len(w_act)
    acts = refs[:n_acts]
    ws = refs[n_acts:n_acts + nw]
    extras = refs[n_acts + nw:n_acts + nw + n_extras]
    rest = refs[n_acts + nw + n_extras:]
    if cast:
        out, wbfs = rest[0], rest[1:]

        @pl.when(pl.program_id(1) == 0)
        def _():
            for w, wb in zip(ws, wbfs):
                wb[...] = w[...].astype(BF16)
        ws = wbfs
    else:
        out = rest[-1]
    a_vals = [a[...] for a in acts]
    dots = [jnp.dot(a_vals[ai], w[...], preferred_element_type=F32) for ai, w in zip(w_act, ws)]
    out[...] = epilogue(dots, [e[...] for e in extras]).astype(out.dtype)


def _fused_linear(name, layer, acts, weights, extras, epilogue, out_dtype, n_out, *,
                  tm, tn, head_tm, head_tn, head_tiles, n_temps):
    m = acts[0].shape[0]
    w_act = tuple(ai for _, ai, _ in weights)
    ks = [w.shape[1] for w, _, _ in weights]
    kern = functools.partial(_linear_kernel, w_act=w_act, n_acts=len(acts), n_extras=len(extras),
                             epilogue=epilogue)

    nj = n_out // head_tn
    act_mode = dict(pipeline_mode=pl.Buffered(1)) if head_tiles == 1 else {}
    in_specs = [pl.BlockSpec((head_tm, a.shape[1]), lambda j, i: (i, 0), **act_mode) for a in acts]
    in_specs += [pl.BlockSpec((None, k, head_tn), lambda j, i, off=c0 // head_tn: (layer, 0, off + j))
                 for k, (_, _, c0) in zip(ks, weights)]
    in_specs += [pl.BlockSpec((head_tm, head_tn), lambda j, i: (i, j)) for _ in extras]
    out_specs = [pl.BlockSpec((head_tm, head_tn), lambda j, i: (i, j))]
    out_specs += [pl.BlockSpec((k, head_tn), lambda j, i: (0, j)) for k in ks]
    out_shape = [jax.ShapeDtypeStruct((m, n_out), out_dtype)]
    out_shape += [jax.ShapeDtypeStruct((k, n_out), BF16) for k in ks]
    act_blocks = [((head_tm, a.shape[1]), BF16) for a in acts]
    blocks = [((k, head_tn), F32) for k in ks] + [((k, head_tn), BF16) for k in ks]
    blocks += [((head_tm, head_tn), F32)] * len(extras) + [((head_tm, head_tn), out_dtype)]
    single = act_blocks if head_tiles == 1 else []
    blocks += [] if head_tiles == 1 else act_blocks
    res = pl.pallas_call(
        functools.partial(kern, cast=True),
        grid=(nj, head_tiles),
        in_specs=in_specs, out_specs=out_specs, out_shape=out_shape,
        compiler_params=_params(("arbitrary", "arbitrary"),
                                _vmem_limit(blocks, single=single,
                                            temps=[((head_tm, head_tn), F32)] * n_temps)),
        name=name + "_head",
    )(*acts, *[w for w, _, _ in weights], *extras)
    out, wbfs = res[0], res[1:]

    r0 = head_tm * head_tiles // tm
    n_in = len(acts) + len(weights) + len(extras)
    in_specs = [pl.BlockSpec((tm, a.shape[1]), lambda i, j: (i + r0, 0)) for a in acts]
    in_specs += [pl.BlockSpec((k, tn), lambda i, j: (0, j)) for k in ks]
    in_specs += [pl.BlockSpec((tm, tn), lambda i, j: (i + r0, j)) for _ in extras]
    in_specs += [pl.BlockSpec(memory_space=pl.ANY)]
    blocks = [((tm, a.shape[1]), BF16) for a in acts] + [((k, tn), BF16) for k in ks]
    blocks += [((tm, tn), F32)] * len(extras) + [((tm, tn), out_dtype)]
    return pl.pallas_call(
        functools.partial(kern, cast=False),
        grid=(m // tm - r0, n_out // tn),
        in_specs=in_specs,
        out_specs=pl.BlockSpec((tm, tn), lambda i, j: (i + r0, j)),
        out_shape=jax.ShapeDtypeStruct((m, n_out), out_dtype),
        input_output_aliases={n_in: 0},
        compiler_params=_params(("parallel", "parallel"),
                                _vmem_limit(blocks, temps=[((tm, tn), F32)] * n_temps)),
        name=name + "_tail",
    )(*acts, *wbfs, *extras, out)


def _swiglu_ffn(layer, n, w_gate, w_up, w_down, h):
    act = _fused_linear(
        "ffn_gateup", layer, [n], [(w_gate, 0, 0), (w_up, 0, 0)], [],
        lambda d, e: d[0] * jax.nn.sigmoid(d[0]) * d[1], BF16, D_FF,
        tm=2048, tn=256, head_tm=1024, head_tn=256, head_tiles=2, n_temps=4)
    return _fused_linear(
        "ffn_down", layer, [act], [(w_down, 0, 0)], [h],
        lambda d, e: e[0] + 0.5 * d[0], F32, D_MODEL,
        tm=512, tn=512, head_tm=512, head_tn=256, head_tiles=1, n_temps=2)


def _log_sigmoid(x):
    return jnp.minimum(x, 0.0) - jnp.log1p(jnp.exp(-jnp.abs(x)))


def _forget_kernel(u_ref, w_ref, b_ref, c_ref, carry_ref, *, ts):
    @pl.when(pl.program_id(1) == 0)
    def _():
        carry_ref[...] = jnp.zeros_like(carry_ref)

    logit = jnp.dot(u_ref[0], w_ref[...], preferred_element_type=F32) + b_ref[...]
    c = _log_sigmoid(logit)
    row = lax.broadcasted_iota(jnp.int32, c.shape, 0)
    shift = 1
    while shift < ts:
        c = c + jnp.where(row >= shift, pltpu.roll(c, shift, axis=0), 0.0)
        shift *= 2
    c = c + carry_ref[...]
    c_ref[0] = c
    carry_ref[...] = c[ts - 1:ts, :]


def _forget_cumsum(u3, wf, bias, ts=512):
    b, s, d = u3.shape
    blocks = [((1, ts, d), BF16), ((d, LANES), BF16), ((1, LANES), F32), ((1, ts, LANES), F32)]
    return pl.pallas_call(
        functools.partial(_forget_kernel, ts=ts),
        grid=(b, s // ts),
        in_specs=[pl.BlockSpec((1, ts, d), lambda bi, si: (bi, si, 0)),
                  pl.BlockSpec((d, LANES), lambda bi, si: (0, 0)),
                  pl.BlockSpec((1, LANES), lambda bi, si: (0, 0))],
        out_specs=pl.BlockSpec((1, ts, LANES), lambda bi, si: (bi, si, 0)),
        out_shape=jax.ShapeDtypeStruct((b, s, LANES), F32),
        scratch_shapes=[pltpu.VMEM((1, LANES), F32)],
        compiler_params=_params(("parallel", "arbitrary"),
                                _vmem_limit(blocks, temps=[((ts, LANES), F32)] * 8)),
        name="forget_cumsum",
    )(u3, wf, bias)


_HPS = 2
_HW = _HPS * HEAD_DIM


def _head_cols(hh):
    return slice(hh * HEAD_DIM, (hh + 1) * HEAD_DIM)


def _fox_kernel(q_ref, k_ref, v_ref, c_ref, ct_ref, o_ref, *, tq):
    hp = pl.program_id(1)
    i = pl.program_id(2)
    q_start = pl.multiple_of(i * tq, tq)
    lane = lax.broadcasted_iota(jnp.int32, (tq, LANES), 1)
    q = [(q_ref[0, :, _head_cols(hh)].astype(F32) * Q_SCALE).astype(BF16)
         for hh in range(_HPS)]
    c_t = [ct_ref[0, hh, :, pl.ds(q_start, tq)] * LOG2E for hh in range(_HPS)]

    def block(j, carry, hh, masked):
        m, l, acc = carry
        start = pl.multiple_of(j * tq, tq)
        k = k_ref[0, pl.ds(start, tq), _head_cols(hh)]
        v = v_ref[0, pl.ds(start, tq), _head_cols(hh)]
        c_s = jnp.sum(jnp.where(lane == hp * _HPS + hh, c_ref[0, pl.ds(start, tq), :], 0.0),
                      axis=1, keepdims=True)
        x = lax.dot_general(k, q[hh], _NT, preferred_element_type=F32) - c_s * LOG2E
        if masked:
            key = lax.broadcasted_iota(jnp.int32, (tq, tq), 0)
            qry = lax.broadcasted_iota(jnp.int32, (tq, tq), 1)
            x = jnp.where(key <= qry, x, -jnp.inf)
        m_new = jnp.maximum(m, jnp.max(x, axis=0, keepdims=True) + c_t[hh])
        p = jnp.exp2(x + (c_t[hh] - m_new))
        alpha = jnp.exp2(m - m_new)
        l = alpha * l + jnp.sum(p, axis=0, keepdims=True)
        acc = alpha * acc + lax.dot_general(v, p.astype(BF16), _TN, preferred_element_type=F32)
        return m_new, l, acc

    def blocks(j, carries, masked):
        return tuple(block(j, carries[hh], hh, masked) for hh in range(_HPS))

    init = (jnp.full((1, tq), -jnp.inf, F32), jnp.zeros((1, tq), F32),
            jnp.zeros((HEAD_DIM, tq), F32))
    carries = lax.fori_loop(0, i, lambda j, c: blocks(j, c, False), (init,) * _HPS)
    carries = blocks(i, carries, True)
    for hh, (_, l, acc) in enumerate(carries):
        o_ref[0, :, _head_cols(hh)] = (acc / l).T.astype(o_ref.dtype)


def _fox_attention(z3, c, ct, tq=512):
    b, s, _ = z3.shape
    npair = N_HEADS // _HPS
    blocks = [((1, tq, _HW), BF16), ((1, s, _HW), BF16), ((1, s, _HW), BF16),
              ((1, s, LANES), F32), ((1, _HPS, 1, s), F32), ((1, tq, _HW), BF16)]
    return pl.pallas_call(
        functools.partial(_fox_kernel, tq=tq),
        grid=(b, npair, s // tq),
        in_specs=[pl.BlockSpec((1, tq, _HW), lambda bi, hp, i: (bi, i, hp)),
                  pl.BlockSpec((1, s, _HW), lambda bi, hp, i: (bi, 0, npair + hp)),
                  pl.BlockSpec((1, s, _HW), lambda bi, hp, i: (bi, 0, 2 * npair + hp)),
                  pl.BlockSpec((1, s, LANES), lambda bi, hp, i: (bi, 0, 0)),
                  pl.BlockSpec((1, _HPS, 1, s), lambda bi, hp, i: (bi, hp, 0, 0))],
        out_specs=pl.BlockSpec((1, tq, _HW), lambda bi, hp, i: (bi, i, hp)),
        out_shape=jax.ShapeDtypeStruct((b, s, WIDTH), BF16),
        compiler_params=_params(("parallel", "parallel", "arbitrary"),
                                _vmem_limit(blocks, temps=[((tq, tq), F32)] * 6 * _HPS)),
        name="fox_attention",
    )(z3, z3, z3, c, ct)


_QB = 8 * CHUNK
_N_BIAS_VEC = 12


def _bias_pieces_index():
    e = 128 * (np.arange(_N_BIAS_VEC)[:, None] - 4) + np.arange(LANES)[None, :]
    dist = _QB - e
    return np.clip(dist, -REL_CLIP, REL_CLIP) + REL_CLIP


def _chunk_kernel(q_ref, kp_ref, kc_ref, vp_ref, vc_ref, g_ref, o_ref, bias_ref):
    bi = pl.program_id(1)
    i = pl.program_id(2)

    @pl.when((bi == 0) & (i == 0))
    def _():
        r = lax.broadcasted_iota(jnp.int32, (LANES, LANES), 0)
        col = lax.broadcasted_iota(jnp.int32, (LANES, LANES), 1)
        upper = col >= r
        for hh in range(_HPS):
            rolled = [pltpu.roll(jnp.broadcast_to(g_ref[hh, k:k + 1, :] * LOG2E, (LANES, LANES)),
                                 0, 1, stride=1, stride_axis=0) for k in range(_N_BIAS_VEC)]
            for rb in range(_QB // LANES):
                for cb in range(2 * _QB // LANES):
                    delta = cb - rb + 4
                    tile = jnp.where(upper, rolled[delta], rolled[delta - 1])
                    q_chunk = (rb * LANES + r) // CHUNK
                    k_chunk = (cb * LANES + col) // CHUNK
                    ok = (k_chunk >= q_chunk) & (k_chunk <= q_chunk + LEFT_CHUNKS)
                    bias_ref[hh, cb * LANES:(cb + 1) * LANES, rb * LANES:(rb + 1) * LANES] = (
                        jnp.where(ok, tile, -jnp.inf).T)

    for hh in range(_HPS):
        cols = _head_cols(hh)
        q = (q_ref[0, :, cols].astype(F32) * Q_SCALE).astype(BF16)
        x_prev = (lax.dot_general(kp_ref[0, :, cols], q, _NT, preferred_element_type=F32)
                  + bias_ref[hh, :_QB, :])
        x_prev = jnp.where(i > 0, x_prev, -jnp.inf)
        x_cur = (lax.dot_general(kc_ref[0, :, cols], q, _NT, preferred_element_type=F32)
                 + bias_ref[hh, _QB:, :])
        m = jnp.maximum(jnp.max(x_prev, axis=0, keepdims=True),
                        jnp.max(x_cur, axis=0, keepdims=True))
        p_prev = jnp.exp2(x_prev - m)
        p_cur = jnp.exp2(x_cur - m)
        l = jnp.sum(p_prev, axis=0, keepdims=True) + jnp.sum(p_cur, axis=0, keepdims=True)
        acc = lax.dot_general(vp_ref[0, :, cols], p_prev.astype(BF16), _TN,
                              preferred_element_type=F32)
        acc = acc + lax.dot_general(vc_ref[0, :, cols], p_cur.astype(BF16), _TN,
                                    preferred_element_type=F32)
        o_ref[0, :, cols] = (acc / l).T.astype(o_ref.dtype)


def _chunk_attention(z3, bias_pieces):
    b, s, _ = z3.shape
    nb = s // _QB
    npair = N_HEADS // _HPS
    qo, ko, vo = 3 * npair, 4 * npair, 5 * npair
    blk = (1, _QB, _HW)
    blocks = [(blk, BF16)] * 6 + [((_HPS, 16, LANES), F32)]
    prev = lambda i: jnp.maximum(i - 1, 0)
    return pl.pallas_call(
        _chunk_kernel,
        grid=(npair, b, nb),
        in_specs=[pl.BlockSpec(blk, lambda hp, bi, i: (bi, i, qo + hp)),
                  pl.BlockSpec(blk, lambda hp, bi, i: (bi, prev(i), ko + hp)),
                  pl.BlockSpec(blk, lambda hp, bi, i: (bi, i, ko + hp)),
                  pl.BlockSpec(blk, lambda hp, bi, i: (bi, prev(i), vo + hp)),
                  pl.BlockSpec(blk, lambda hp, bi, i: (bi, i, vo + hp)),
                  pl.BlockSpec((_HPS, 16, LANES), lambda hp, bi, i: (hp, 0, 0))],
        out_specs=pl.BlockSpec(blk, lambda hp, bi, i: (bi, i, hp)),
        out_shape=jax.ShapeDtypeStruct((b, s, WIDTH), BF16),
        scratch_shapes=[pltpu.VMEM((_HPS, 2 * _QB, _QB), F32)],
        compiler_params=_params(("arbitrary", "arbitrary", "arbitrary"),
                                _vmem_limit(blocks, single=[((_HPS, 2 * _QB, _QB), F32)],
                                            temps=[((_QB, _QB), F32)] * 8 * _HPS)),
        name="chunk_attention",
    )(z3, z3, z3, z3, z3, bias_pieces)


def kernel(x, p, ffn1_norm, ffn1_w_gate, ffn1_w_up, ffn1_w_down, mix_norm, w_in, fox_forget_bias,
           rel_bias, w_branch_gate, w_proj_a, w_proj_b, w_out, ffn2_norm, ffn2_w_gate, ffn2_w_up,
           ffn2_w_down, ple_norm, ple_w_gate, ple_w_proj, final_norm):
    w_forget = jnp.pad(w_in[:, :, N_QKV:], ((0, 0), (0, 0), (0, LANES - N_HEADS))).astype(BF16)
    forget_bias = jnp.pad(fox_forget_bias, ((0, 0), (0, LANES - N_HEADS)))[:, None, :]
    p_bf = p.astype(BF16).reshape(DEPTH, M_TOK, PLE_DIM)
    bias_pieces = jnp.pad(rel_bias[:, :, _bias_pieces_index()],
                          ((0, 0), (0, 0), (0, 16 - _N_BIAS_VEC), (0, 0)))
    sig = jax.nn.sigmoid

    h = x.reshape(M_TOK, D_MODEL)
    for i in range(DEPTH):
        n = _rmsnorm(h, ffn1_norm[i], BF16)
        h = _swiglu_ffn(i, n, ffn1_w_gate, ffn1_w_up, ffn1_w_down, h)

        u = _rmsnorm(h, mix_norm[i], BF16)
        z = _fused_linear("qkv_proj", i, [u], [(w_in, 0, 0)], [], lambda d, e: d[0], BF16, N_QKV,
                          tm=1024, tn=1024, head_tm=1024, head_tn=512, head_tiles=1, n_temps=2)
        z3 = z.reshape(BATCH, SEQ, N_QKV)
        c = _forget_cumsum(u.reshape(BATCH, SEQ, D_MODEL), w_forget[i], forget_bias[i])
        ct = c[:, :, :N_HEADS].transpose(0, 2, 1)[:, :, None, :]
        attn_a = _fox_attention(z3, c, ct).reshape(M_TOK, WIDTH)
        attn_b = _chunk_attention(z3, bias_pieces[i]).reshape(M_TOK, WIDTH)
        mix = _fused_linear(
            "branch_mix", i, [u, attn_a, attn_b],
            [(w_branch_gate, 0, 0), (w_branch_gate, 0, D_MODEL), (w_proj_a, 1, 0), (w_proj_b, 2, 0)],
            [], lambda d, e: sig(d[0]) * d[2] + sig(d[1]) * d[3], BF16, D_MODEL,
            tm=1024, tn=256, head_tm=256, head_tn=256, head_tiles=4, n_temps=6)
        h = _fused_linear("out_proj", i, [mix], [(w_out, 0, 0)], [h], lambda d, e: e[0] + d[0],
                          F32, D_MODEL,
                          tm=1024, tn=512, head_tm=1024, head_tn=512, head_tiles=1, n_temps=2)

        n = _rmsnorm(h, ffn2_norm[i], BF16)
        h = _swiglu_ffn(i, n, ffn2_w_gate, ffn2_w_up, ffn2_w_down, h)

        n = _rmsnorm(h, ple_norm[i], BF16)
        h = _fused_linear("ple", i, [n, p_bf[i]], [(ple_w_gate, 0, 0), (ple_w_proj, 1, 0)], [h],
                          lambda d, e: e[0] + sig(d[0]) * d[1], F32, D_MODEL,
                          tm=1024, tn=512, head_tm=1024, head_tn=512, head_tiles=1, n_temps=4)
    out = _rmsnorm(h, final_norm, F32)
    return out.reshape(BATCH, SEQ, D_MODEL)
```

```python
import functools
import math

import jax
import jax.numpy as jnp
import numpy as np
from jax import lax
from jax.experimental import pallas as pl
from jax.experimental.pallas import tpu as pltpu

D_MODEL = 4096
BATCH = 4
SEQ = 2048
DEPTH = 2
CHUNK = 64
PLE_DIM = 256
D_FF = 11008
HEAD_DIM = 128
N_HEADS = D_MODEL // (2 * HEAD_DIM)
WIDTH = N_HEADS * HEAD_DIM
LEFT_CHUNKS = 8
REL_CLIP = 128
RMS_EPS = 1e-6
N_QKV = 6 * WIDTH
M_TOK = BATCH * SEQ
LOG2E = math.log2(math.e)
Q_SCALE = HEAD_DIM ** -0.5 * LOG2E

F32 = jnp.float32
BF16 = jnp.bfloat16

LANES = 128
V7X_VMEM_REQUEST_CAP = 60000 * 1024
MIB = 1024 * 1024

_NT = (((1,), (1,)), ((), ()))
_TN = (((0,), (0,)), ((), ()))


def _nbytes(shape, dtype):
    return int(np.prod([s for s in shape if s is not None])) * jnp.dtype(dtype).itemsize


def _vmem_limit(blocks, single=(), temps=()):
    total = 2 * sum(_nbytes(s, d) for s, d in blocks)
    total += sum(_nbytes(s, d) for s, d in single)
    total += sum(_nbytes(s, d) for s, d in temps)
    total += 4 * MIB
    return min(total, V7X_VMEM_REQUEST_CAP)


def _params(semantics, limit):
    return pltpu.CompilerParams(dimension_semantics=semantics, vmem_limit_bytes=limit)


def _rmsnorm_kernel(h_ref, g_ref, o_ref):
    x = h_ref[...]
    ms = jnp.mean(x * x, axis=-1, keepdims=True)
    o_ref[...] = (x * lax.rsqrt(ms + RMS_EPS) * g_ref[...]).astype(o_ref.dtype)


def _rmsnorm(h, g, out_dtype, tr=512):
    m, d = h.shape
    blocks = [((tr, d), F32), ((1, d), F32), ((tr, d), out_dtype)]
    return pl.pallas_call(
        _rmsnorm_kernel,
        grid=(m // tr,),
        in_specs=[pl.BlockSpec((tr, d), lambda i: (i, 0)),
                  pl.BlockSpec((1, d), lambda i: (0, 0))],
        out_specs=pl.BlockSpec((tr, d), lambda i: (i, 0)),
        out_shape=jax.ShapeDtypeStruct((m, d), out_dtype),
        compiler_params=_params(("parallel",), _vmem_limit(blocks, temps=[((tr, d), F32)] * 2)),
        name="rmsnorm",
    )(h, g.reshape(1, d))


def _row_scale(ssq):
    return lax.rsqrt(ssq * (1.0 / D_MODEL) + RMS_EPS)


def _prenorm_kernel(h_ref, g_ref, hb_ref, ssq_ref):
    x = h_ref[...]
    hb_ref[...] = (x * g_ref[...]).astype(BF16)
    ssq_ref[...] = jnp.sum(x * x, axis=-1, keepdims=True)


def _prenorm(h, g, tr=512):
    m, d = h.shape
    blocks = [((tr, d), F32), ((1, d), F32), ((tr, d), BF16), ((tr, LANES), F32)]
    return pl.pallas_call(
        _prenorm_kernel,
        grid=(m // tr,),
        in_specs=[pl.BlockSpec((tr, d), lambda i: (i, 0)),
                  pl.BlockSpec((1, d), lambda i: (0, 0))],
        out_specs=[pl.BlockSpec((tr, d), lambda i: (i, 0)),
                   pl.BlockSpec((tr, 1), lambda i: (i, 0))],
        out_shape=[jax.ShapeDtypeStruct((m, d), BF16), jax.ShapeDtypeStruct((m, 1), F32)],
        compiler_params=_params(("parallel",), _vmem_limit(blocks, temps=[((tr, d), F32)] * 2)),
        name="prenorm",
    )(h, g.reshape(1, d))


def _linear_kernel(*refs, w_act, n_acts, n_extras, epilogue, cast, scaled, stats, n_alias, col_axis,
                   w_rows_are_outputs):
    nw = len(w_act)
    it = iter(refs)
    take = lambda n: [next(it) for _ in range(n)]
    acts, ws, extras = take(n_acts), take(nw), take(n_extras)
    ssq_in = take(1)[0] if scaled else None
    gain = take(1)[0] if stats else None
    take(n_alias)
    out = take(1)[0]
    hb, ssq_out = take(2) if stats else (None, None)
    if cast:
        wbfs = take(nw)
        for w, wb in zip(ws, wbfs):
            wb[...] = w[...].astype(BF16)
        ws = wbfs
    a_vals = [a[...] for a in acts]
    dims = _NT if w_rows_are_outputs else (((1,), (0,)), ((), ()))
    dots = [lax.dot_general(a_vals[ai], w[...], dims, preferred_element_type=F32)
            for ai, w in zip(w_act, ws)]
    rs = _row_scale(ssq_in[...]) if scaled else None
    res = epilogue(dots, [e[...] for e in extras], rs)
    out[...] = res.astype(out.dtype)
    if stats:
        hb[...] = (res * gain[...]).astype(BF16)
        part = jnp.sum(res * res, axis=1, keepdims=True)
        col = pl.program_id(col_axis)

        @pl.when(col == 0)
        def _():
            ssq_out[...] = part

        @pl.when(col != 0)
        def _():
            ssq_out[...] += part


def _fused_linear(name, layer, acts, weights, extras, epilogue, out_dtype, n_out, *,
                  tm, tn, head_tm, head_tn, n_temps, row_ssq=None, next_gain=None,
                  w_rows_are_outputs=False):
    m = acts[0].shape[0]
    w_act = tuple(ai for _, ai, _ in weights)
    wt = w_rows_are_outputs
    ks = [w.shape[2 if wt else 1] for w, _, _ in weights]
    scaled, stats = row_ssq is not None, next_gain is not None
    kern = functools.partial(_linear_kernel, w_act=w_act, n_acts=len(acts), n_extras=len(extras),
                             epilogue=epilogue, scaled=scaled, stats=stats, w_rows_are_outputs=wt)
    w_blk = lambda k, n: (n, k) if wt else (k, n)
    w_idx = lambda c: (c, 0) if wt else (0, c)
    side_in = ([row_ssq] if scaled else []) + ([next_gain] if stats else [])

    def specs(tm_, tn_, row, colblk):
        ins = [pl.BlockSpec((tm_, tn_), lambda *g: (row(*g), colblk(*g))) for _ in extras]
        ins += [pl.BlockSpec((tm_, 1), lambda *g: (row(*g), 0))] if scaled else []
        ins += [pl.BlockSpec((1, tn_), lambda *g: (0, colblk(*g)))] if stats else []
        outs = [pl.BlockSpec((tm_, tn_), lambda *g: (row(*g), colblk(*g)))]
        shapes = [jax.ShapeDtypeStruct((m, n_out), out_dtype)]
        blocks = [((tm_, tn_), F32)] * len(extras) + [((tm_, tn_), out_dtype)]
        if stats:
            outs += [pl.BlockSpec((tm_, tn_), lambda *g: (row(*g), colblk(*g))),
                     pl.BlockSpec((tm_, 1), lambda *g: (row(*g), 0))]
            shapes += [jax.ShapeDtypeStruct((m, n_out), BF16), jax.ShapeDtypeStruct((m, 1), F32)]
            blocks += [((tm_, tn_), BF16), ((tm_, LANES), F32)]
        blocks += [((tm_, LANES), F32)] if scaled else []
        return ins, outs, shapes, blocks

    for e in extras:
        assert e.shape[1] == n_out
    ins, outs, shapes, blocks = specs(head_tm, head_tn, lambda j: 0, lambda j: j)
    in_specs = [pl.BlockSpec((head_tm, a.shape[1]), lambda j: (0, 0), pipeline_mode=pl.Buffered(1))
                for a in acts]
    in_specs += [pl.BlockSpec((None,) + w_blk(k, head_tn),
                              lambda j, off=c0 // head_tn: (layer,) + w_idx(off + j))
                 for k, (_, _, c0) in zip(ks, weights)]
    outs += [pl.BlockSpec(w_blk(k, head_tn), lambda j: w_idx(j)) for k in ks]
    shapes += [jax.ShapeDtypeStruct(w_blk(k, n_out), BF16) for k in ks]
    blocks += [((k, head_tn), F32) for k in ks] + [((k, head_tn), BF16) for k in ks]
    res = pl.pallas_call(
        functools.partial(kern, cast=True, n_alias=0, col_axis=0),
        grid=(n_out // head_tn,),
        in_specs=in_specs + ins, out_specs=outs, out_shape=shapes,
        compiler_params=_params(("arbitrary",),
                                _vmem_limit(blocks, single=[((head_tm, a.shape[1]), BF16) for a in acts],
                                            temps=[((head_tm, head_tn), F32)] * n_temps)),
        name=name + "_head",
    )(*acts, *[w for w, _, _ in weights], *extras, *side_in)
    n_res = 3 if stats else 1
    prior, wbfs = res[:n_res], res[n_res:]

    r0 = head_tm // tm
    ins, outs, shapes, blocks = specs(tm, tn, lambda i, j: i + r0, lambda i, j: j)
    in_specs = [pl.BlockSpec((tm, a.shape[1]), lambda i, j: (i + r0, 0)) for a in acts]
    in_specs += [pl.BlockSpec(w_blk(k, tn), lambda i, j: w_idx(j)) for k in ks]
    n_in = len(in_specs) + len(ins)
    blocks += [((tm, a.shape[1]), BF16) for a in acts] + [((k, tn), BF16) for k in ks]
    res = pl.pallas_call(
        functools.partial(kern, cast=False, n_alias=n_res, col_axis=1),
        grid=(m // tm - r0, n_out // tn),
        in_specs=in_specs + ins + [pl.BlockSpec(memory_space=pl.ANY)] * n_res,
        out_specs=outs, out_shape=shapes,
        input_output_aliases={n_in + r: r for r in range(n_res)},
        compiler_params=_params(("parallel", "arbitrary"),
                                _vmem_limit(blocks, temps=[((tm, tn), F32)] * n_temps)),
        name=name + "_tail",
    )(*acts, *wbfs, *extras, *side_in, *prior)
    return tuple(res) if stats else res[0]


def _swiglu_ffn(layer, hb, ssq, w_gate, w_up, w_down, h, next_gain):
    def gate_up(d, e, rs):
        g = d[0] * rs
        return g * jax.nn.sigmoid(g) * (d[1] * rs)

    act = _fused_linear("ffn_gateup", layer, [hb], [(w_gate, 0, 0), (w_up, 0, 0)], [], gate_up,
                        BF16, D_FF, tm=2048, tn=256, head_tm=2048, head_tn=256, n_temps=4,
                        row_ssq=ssq)
    return _fused_linear("ffn_down", layer, [act], [(w_down, 0, 0)], [h],
                         lambda d, e, rs: e[0] + 0.5 * d[0], F32, D_MODEL,
                         tm=512, tn=512, head_tm=512, head_tn=256, n_temps=2, next_gain=next_gain)


def _log_sigmoid(x):
    return jnp.minimum(x, 0.0) - jnp.log1p(jnp.exp(-jnp.abs(x)))


def _forget_kernel(u_ref, ssq_ref, w_ref, b_ref, c_ref, carry_ref, *, ts):
    @pl.when(pl.program_id(1) == 0)
    def _():
        carry_ref[...] = jnp.zeros_like(carry_ref)

    w = w_ref[...].astype(BF16)
    w = jnp.concatenate([w, jnp.zeros((LANES - N_HEADS, w.shape[1]), BF16)], axis=0)
    logit = lax.dot_general(u_ref[0], w, _NT, preferred_element_type=F32)
    logit = logit * _row_scale(ssq_ref[0]) + b_ref[...]
    c = _log_sigmoid(logit)
    row = lax.broadcasted_iota(jnp.int32, c.shape, 0)
    shift = 1
    while shift < ts:
        c = c + jnp.where(row >= shift, pltpu.roll(c, shift, axis=0), 0.0)
        shift *= 2
    c = c + carry_ref[...]
    c_ref[0] = c
    carry_ref[...] = c[ts - 1:ts, :]


def _forget_cumsum(layer, u3, ssq3, w_in_t, bias, ts=512):
    b, s, d = u3.shape
    blocks = [((1, ts, d), BF16), ((1, ts, LANES), F32), ((N_HEADS, d), F32), ((1, LANES), F32),
              ((1, ts, LANES), F32)]
    return pl.pallas_call(
        functools.partial(_forget_kernel, ts=ts),
        grid=(b, s // ts),
        in_specs=[pl.BlockSpec((1, ts, d), lambda bi, si: (bi, si, 0)),
                  pl.BlockSpec((1, ts, 1), lambda bi, si: (bi, si, 0)),
                  pl.BlockSpec((None, N_HEADS, d), lambda bi, si: (layer, N_QKV // N_HEADS, 0)),
                  pl.BlockSpec((1, LANES), lambda bi, si: (0, 0))],
        out_specs=pl.BlockSpec((1, ts, LANES), lambda bi, si: (bi, si, 0)),
        out_shape=jax.ShapeDtypeStruct((b, s, LANES), F32),
        scratch_shapes=[pltpu.VMEM((1, LANES), F32)],
        compiler_params=_params(("parallel", "arbitrary"),
                                _vmem_limit(blocks, temps=[((ts, LANES), F32)] * 8 + [((LANES, d), BF16)])),
        name="forget_cumsum",
    )(u3, ssq3, w_in_t, bias)


_HPS = 2
_HW = _HPS * HEAD_DIM


def _head_cols(hh):
    return slice(hh * HEAD_DIM, (hh + 1) * HEAD_DIM)


def _fox_kernel(q_ref, k_ref, v_ref, c_ref, ct_ref, o_ref, *, tq):
    hp = pl.program_id(1)
    i = pl.program_id(2)
    q_start = pl.multiple_of(i * tq, tq)
    lane = lax.broadcasted_iota(jnp.int32, (tq, LANES), 1)
    q = [(q_ref[0, :, _head_cols(hh)].astype(F32) * Q_SCALE).astype(BF16)
         for hh in range(_HPS)]
    c_t = [ct_ref[0, hh, :, pl.ds(q_start, tq)] * LOG2E for hh in range(_HPS)]

    def block(j, carry, hh, masked):
        m, l, acc = carry
        start = pl.multiple_of(j * tq, tq)
        k = k_ref[0, pl.ds(start, tq), _head_cols(hh)]
        v = v_ref[0, pl.ds(start, tq), _head_cols(hh)]
        c_s = jnp.sum(jnp.where(lane == hp * _HPS + hh, c_ref[0, pl.ds(start, tq), :], 0.0),
                      axis=1, keepdims=True)
        x = lax.dot_general(k, q[hh], _NT, preferred_element_type=F32) - c_s * LOG2E
        if masked:
            key = lax.broadcasted_iota(jnp.int32, (tq, tq), 0)
            qry = lax.broadcasted_iota(jnp.int32, (tq, tq), 1)
            x = jnp.where(key <= qry, x, -jnp.inf)
        m_new = jnp.maximum(m, jnp.max(x, axis=0, keepdims=True) + c_t[hh])
        p = jnp.exp2(x + (c_t[hh] - m_new))
        alpha = jnp.exp2(m - m_new)
        l = alpha * l + jnp.sum(p, axis=0, keepdims=True)
        acc = alpha * acc + lax.dot_general(v, p.astype(BF16), _TN, preferred_element_type=F32)
        return m_new, l, acc

    def blocks(j, carries, masked):
        return tuple(block(j, carries[hh], hh, masked) for hh in range(_HPS))

    init = (jnp.full((1, tq), -jnp.inf, F32), jnp.zeros((1, tq), F32),
            jnp.zeros((HEAD_DIM, tq), F32))
    carries = lax.fori_loop(0, i, lambda j, c: blocks(j, c, False), (init,) * _HPS)
    carries = blocks(i, carries, True)
    for hh, (_, l, acc) in enumerate(carries):
        o_ref[0, :, _head_cols(hh)] = (acc / l).T.astype(o_ref.dtype)


def _fox_attention(z3, c, ct, tq=512):
    b, s, _ = z3.shape
    npair = N_HEADS // _HPS
    blocks = [((1, tq, _HW), BF16), ((1, s, _HW), BF16), ((1, s, _HW), BF16),
              ((1, s, LANES), F32), ((1, _HPS, 1, s), F32), ((1, tq, _HW), BF16)]
    return pl.pallas_call(
        functools.partial(_fox_kernel, tq=tq),
        grid=(b, npair, s // tq),
        in_specs=[pl.BlockSpec((1, tq, _HW), lambda bi, hp, i: (bi, i, hp)),
                  pl.BlockSpec((1, s, _HW), lambda bi, hp, i: (bi, 0, npair + hp)),
                  pl.BlockSpec((1, s, _HW), lambda bi, hp, i: (bi, 0, 2 * npair + hp)),
                  pl.BlockSpec((1, s, LANES), lambda bi, hp, i: (bi, 0, 0)),
                  pl.BlockSpec((1, _HPS, 1, s), lambda bi, hp, i: (bi, hp, 0, 0))],
        out_specs=pl.BlockSpec((1, tq, _HW), lambda bi, hp, i: (bi, i, hp)),
        out_shape=jax.ShapeDtypeStruct((b, s, WIDTH), BF16),
        compiler_params=_params(("parallel", "parallel", "arbitrary"),
                                _vmem_limit(blocks, temps=[((tq, tq), F32)] * 6 * _HPS)),
        name="fox_attention",
    )(z3, z3, z3, c, ct)


_QB = 8 * CHUNK
_N_BIAS_VEC = 12


def _bias_pieces_index():
    e = 128 * (np.arange(_N_BIAS_VEC)[:, None] - 4) + np.arange(LANES)[None, :]
    dist = _QB - e
    return np.clip(dist, -REL_CLIP, REL_CLIP) + REL_CLIP


def _chunk_kernel(q_ref, kp_ref, kc_ref, vp_ref, vc_ref, g_ref, o_ref, bias_ref):
    bi = pl.program_id(1)
    i = pl.program_id(2)

    @pl.when((bi == 0) & (i == 0))
    def _():
        r = lax.broadcasted_iota(jnp.int32, (LANES, LANES), 0)
        col = lax.broadcasted_iota(jnp.int32, (LANES, LANES), 1)
        upper = col >= r
        for hh in range(_HPS):
            rolled = [pltpu.roll(jnp.broadcast_to(g_ref[hh, k:k + 1, :] * LOG2E, (LANES, LANES)),
                                 0, 1, stride=1, stride_axis=0) for k in range(_N_BIAS_VEC)]
            for rb in range(_QB // LANES):
                for cb in range(2 * _QB // LANES):
                    delta = cb - rb + 4
                    tile = jnp.where(upper, rolled[delta], rolled[delta - 1])
                    q_chunk = (rb * LANES + r) // CHUNK
                    k_chunk = (cb * LANES + col) // CHUNK
                    ok = (k_chunk >= q_chunk) & (k_chunk <= q_chunk + LEFT_CHUNKS)
                    bias_ref[hh, cb * LANES:(cb + 1) * LANES, rb * LANES:(rb + 1) * LANES] = (
                        jnp.where(ok, tile, -jnp.inf).T)

    for hh in range(_HPS):
        cols = _head_cols(hh)
        q = (q_ref[0, :, cols].astype(F32) * Q_SCALE).astype(BF16)
        x_prev = (lax.dot_general(kp_ref[0, :, cols], q, _NT, preferred_element_type=F32)
                  + bias_ref[hh, :_QB, :])
        x_prev = jnp.where(i > 0, x_prev, -jnp.inf)
        x_cur = (lax.dot_general(kc_ref[0, :, cols], q, _NT, preferred_element_type=F32)
                 + bias_ref[hh, _QB:, :])
        m = jnp.maximum(jnp.max(x_prev, axis=0, keepdims=True),
                        jnp.max(x_cur, axis=0, keepdims=True))
        p_prev = jnp.exp2(x_prev - m)
        p_cur = jnp.exp2(x_cur - m)
        l = jnp.sum(p_prev, axis=0, keepdims=True) + jnp.sum(p_cur, axis=0, keepdims=True)
        acc = lax.dot_general(vp_ref[0, :, cols], p_prev.astype(BF16), _TN,
                              preferred_element_type=F32)
        acc = acc + lax.dot_general(vc_ref[0, :, cols], p_cur.astype(BF16), _TN,
                                    preferred_element_type=F32)
        o_ref[0, :, cols] = (acc / l).T.astype(o_ref.dtype)


def _chunk_attention(z3, bias_pieces):
    b, s, _ = z3.shape
    nb = s // _QB
    npair = N_HEADS // _HPS
    qo, ko, vo = 3 * npair, 4 * npair, 5 * npair
    blk = (1, _QB, _HW)
    blocks = [(blk, BF16)] * 6 + [((_HPS, 16, LANES), F32)]
    prev = lambda i: jnp.maximum(i - 1, 0)
    return pl.pallas_call(
        _chunk_kernel,
        grid=(npair, b, nb),
        in_specs=[pl.BlockSpec(blk, lambda hp, bi, i: (bi, i, qo + hp)),
                  pl.BlockSpec(blk, lambda hp, bi, i: (bi, prev(i), ko + hp)),
                  pl.BlockSpec(blk, lambda hp, bi, i: (bi, i, ko + hp)),
                  pl.BlockSpec(blk, lambda hp, bi, i: (bi, prev(i), vo + hp)),
                  pl.BlockSpec(blk, lambda hp, bi, i: (bi, i, vo + hp)),
                  pl.BlockSpec((_HPS, 16, LANES), lambda hp, bi, i: (hp, 0, 0))],
        out_specs=pl.BlockSpec(blk, lambda hp, bi, i: (bi, i, hp)),
        out_shape=jax.ShapeDtypeStruct((b, s, WIDTH), BF16),
        scratch_shapes=[pltpu.VMEM((_HPS, 2 * _QB, _QB), F32)],
        compiler_params=_params(("arbitrary", "arbitrary", "arbitrary"),
                                _vmem_limit(blocks, single=[((_HPS, 2 * _QB, _QB), F32)],
                                            temps=[((_QB, _QB), F32)] * 8 * _HPS)),
        name="chunk_attention",
    )(z3, z3, z3, z3, z3, bias_pieces)


def kernel(x, p, ffn1_norm, ffn1_w_gate, ffn1_w_up, ffn1_w_down, mix_norm, w_in, fox_forget_bias,
           rel_bias, w_branch_gate, w_proj_a, w_proj_b, w_out, ffn2_norm, ffn2_w_gate, ffn2_w_up,
           ffn2_w_down, ple_norm, ple_w_gate, ple_w_proj, final_norm):
    forget_bias = jnp.pad(fox_forget_bias, ((0, 0), (0, LANES - N_HEADS)))[:, None, :]
    p_bf = p.astype(BF16).reshape(DEPTH, M_TOK, PLE_DIM)
    bias_pieces = jnp.pad(rel_bias[:, :, _bias_pieces_index()],
                          ((0, 0), (0, 0), (0, 16 - _N_BIAS_VEC), (0, 0)))
    sig = jax.nn.sigmoid
    gain = lambda g: g.reshape(1, D_MODEL)
    w_in_t = jnp.swapaxes(w_in, 1, 2)

    h = x.reshape(M_TOK, D_MODEL)
    hb, ssq = _prenorm(h, ffn1_norm[0])
    for i in range(DEPTH):
        h, hb, ssq = _swiglu_ffn(i, hb, ssq, ffn1_w_gate, ffn1_w_up, ffn1_w_down, h, gain(mix_norm[i]))

        z = _fused_linear("qkv_proj", i, [hb], [(w_in_t, 0, 0)], [], lambda d, e, rs: d[0] * rs,
                          BF16, N_QKV, tm=1024, tn=1024, head_tm=1024, head_tn=512, n_temps=2,
                          row_ssq=ssq, w_rows_are_outputs=True)
        z3 = z.reshape(BATCH, SEQ, N_QKV)
        c = _forget_cumsum(i, hb.reshape(BATCH, SEQ, D_MODEL), ssq.reshape(BATCH, SEQ, 1), w_in_t,
                           forget_bias[i])
        ct = c[:, :, :N_HEADS].transpose(0, 2, 1)[:, :, None, :]
        attn_a = _fox_attention(z3, c, ct).reshape(M_TOK, WIDTH)
        attn_b = _chunk_attention(z3, bias_pieces[i]).reshape(M_TOK, WIDTH)
        mix = _fused_linear(
            "branch_mix", i, [hb, attn_a, attn_b],
            [(w_branch_gate, 0, 0), (w_branch_gate, 0, D_MODEL), (w_proj_a, 1, 0), (w_proj_b, 2, 0)],
            [], lambda d, e, rs: sig(d[0] * rs) * d[2] + sig(d[1] * rs) * d[3], BF16, D_MODEL,
            tm=512, tn=512, head_tm=512, head_tn=256, n_temps=6, row_ssq=ssq)
        h, hb, ssq = _fused_linear("out_proj", i, [mix], [(w_out, 0, 0)], [h],
                                   lambda d, e, rs: e[0] + d[0], F32, D_MODEL,
                                   tm=1024, tn=512, head_tm=1024, head_tn=512, n_temps=2,
                                   next_gain=gain(ffn2_norm[i]))

        h, hb, ssq = _swiglu_ffn(i, hb, ssq, ffn2_w_gate, ffn2_w_up, ffn2_w_down, h, gain(ple_norm[i]))

        last = i + 1 == DEPTH
        res = _fused_linear("ple", i, [hb, p_bf[i]], [(ple_w_gate, 0, 0), (ple_w_proj, 1, 0)], [h],
                            lambda d, e, rs: e[0] + sig(d[0] * rs) * d[1], F32, D_MODEL,
                            tm=1024, tn=512, head_tm=1024, head_tn=512, n_temps=4, row_ssq=ssq,
                            next_gain=None if last else gain(ffn1_norm[i + 1]))
        h, hb, ssq = (res, None, None) if last else res
    out = _rmsnorm(h, final_norm, F32)
    return out.reshape(BATCH, SEQ, D_MODEL)
```

```python
import functools
import math

import jax
import jax.numpy as jnp
import numpy as np
from jax import lax
from jax.experimental import pallas as pl
from jax.experimental.pallas import tpu as pltpu

D_MODEL = 4096
BATCH = 4
SEQ = 2048
DEPTH = 2
CHUNK = 64
PLE_DIM = 256
D_FF = 11008
HEAD_DIM = 128
N_HEADS = D_MODEL // (2 * HEAD_DIM)
WIDTH = N_HEADS * HEAD_DIM
LEFT_CHUNKS = 8
REL_CLIP = 128
RMS_EPS = 1e-6
N_QKV = 6 * WIDTH
M_TOK = BATCH * SEQ
LOG2E = math.log2(math.e)
Q_SCALE = HEAD_DIM ** -0.5 * LOG2E

F32 = jnp.float32
BF16 = jnp.bfloat16

LANES = 128
V7X_VMEM_REQUEST_CAP = 60000 * 1024
MIB = 1024 * 1024

_NT = (((1,), (1,)), ((), ()))
_TN = (((0,), (0,)), ((), ()))


def _nbytes(shape, dtype):
    return int(np.prod([s for s in shape if s is not None])) * jnp.dtype(dtype).itemsize


def _vmem_limit(blocks, single=(), temps=()):
    total = 2 * sum(_nbytes(s, d) for s, d in blocks)
    total += sum(_nbytes(s, d) for s, d in single)
    total += sum(_nbytes(s, d) for s, d in temps)
    total += 4 * MIB
    return min(total, V7X_VMEM_REQUEST_CAP)


def _params(semantics, limit):
    return pltpu.CompilerParams(dimension_semantics=semantics, vmem_limit_bytes=limit)


def _rmsnorm_kernel(h_ref, g_ref, o_ref):
    x = h_ref[...]
    ms = jnp.mean(x * x, axis=-1, keepdims=True)
    o_ref[...] = (x * lax.rsqrt(ms + RMS_EPS) * g_ref[...]).astype(o_ref.dtype)


def _rmsnorm(h, g, out_dtype, tr=512):
    m, d = h.shape
    blocks = [((tr, d), F32), ((1, d), F32), ((tr, d), out_dtype)]
    return pl.pallas_call(
        _rmsnorm_kernel,
        grid=(m // tr,),
        in_specs=[pl.BlockSpec((tr, d), lambda i: (i, 0)),
                  pl.BlockSpec((1, d), lambda i: (0, 0))],
        out_specs=pl.BlockSpec((tr, d), lambda i: (i, 0)),
        out_shape=jax.ShapeDtypeStruct((m, d), out_dtype),
        compiler_params=_params(("parallel",), _vmem_limit(blocks, temps=[((tr, d), F32)] * 2)),
        name="rmsnorm",
    )(h, g.reshape(1, d))


def _row_scale(ssq):
    return lax.rsqrt(ssq * (1.0 / D_MODEL) + RMS_EPS)


def _prenorm_kernel(h_ref, g_ref, hb_ref, ssq_ref):
    x = h_ref[...]
    hb_ref[...] = (x * g_ref[...]).astype(BF16)
    ssq_ref[...] = jnp.sum(x * x, axis=-1, keepdims=True)


def _prenorm(h, g, tr=512):
    m, d = h.shape
    blocks = [((tr, d), F32), ((1, d), F32), ((tr, d), BF16), ((tr, LANES), F32)]
    return pl.pallas_call(
        _prenorm_kernel,
        grid=(m // tr,),
        in_specs=[pl.BlockSpec((tr, d), lambda i: (i, 0)),
                  pl.BlockSpec((1, d), lambda i: (0, 0))],
        out_specs=[pl.BlockSpec((tr, d), lambda i: (i, 0)),
                   pl.BlockSpec((tr, 1), lambda i: (i, 0))],
        out_shape=[jax.ShapeDtypeStruct((m, d), BF16), jax.ShapeDtypeStruct((m, 1), F32)],
        compiler_params=_params(("parallel",), _vmem_limit(blocks, temps=[((tr, d), F32)] * 2)),
        name="prenorm",
    )(h, g.reshape(1, d))


def _linear_kernel(*refs, w_act, n_acts, n_extras, epilogue, cast, scaled, stats, n_alias, col_axis,
                   w_rows_are_outputs, row_chunks):
    nw = len(w_act)
    it = iter(refs)
    take = lambda n: [next(it) for _ in range(n)]
    acts, ws, extras = take(n_acts), take(nw), take(n_extras)
    ssq_in = take(1)[0] if scaled else None
    gain = take(1)[0] if stats else None
    take(n_alias)
    out = take(1)[0]
    hb, ssq_out = take(2) if stats else (None, None)
    if cast:
        wbfs = take(nw)
        for w, wb in zip(ws, wbfs):
            wb[...] = w[...].astype(BF16)
        ws = wbfs
    dims = _NT if w_rows_are_outputs else (((1,), (0,)), ((), ()))
    rc = out.shape[0] // row_chunks
    parts = []
    for c in range(row_chunks):
        rows = slice(c * rc, (c + 1) * rc)
        dots = [lax.dot_general(acts[ai][rows, :], w[...], dims, preferred_element_type=F32)
                for ai, w in zip(w_act, ws)]
        rs = _row_scale(ssq_in[rows, :]) if scaled else None
        res = epilogue(dots, [e[rows, :] for e in extras], rs)
        out[rows, :] = res.astype(out.dtype)
        if stats:
            hb[rows, :] = (res * gain[...]).astype(BF16)
            parts.append(jnp.sum(res * res, axis=1, keepdims=True))
    if stats:
        part = jnp.concatenate(parts, axis=0) if row_chunks > 1 else parts[0]
        col = pl.program_id(col_axis)

        @pl.when(col == 0)
        def _():
            ssq_out[...] = part

        @pl.when(col != 0)
        def _():
            ssq_out[...] += part


def _fused_linear(name, layer, acts, weights, extras, epilogue, out_dtype, n_out, *,
                  tm, tn, head_tm, head_tn, n_temps, row_ssq=None, next_gain=None,
                  w_rows_are_outputs=False, row_chunks=1):
    m = acts[0].shape[0]
    w_act = tuple(ai for _, ai, _ in weights)
    wt = w_rows_are_outputs
    ks = [w.shape[2 if wt else 1] for w, _, _ in weights]
    scaled, stats = row_ssq is not None, next_gain is not None
    kern = functools.partial(_linear_kernel, w_act=w_act, n_acts=len(acts), n_extras=len(extras),
                             epilogue=epilogue, scaled=scaled, stats=stats, w_rows_are_outputs=wt,
                             row_chunks=row_chunks)
    w_blk = lambda k, n: (n, k) if wt else (k, n)
    w_idx = lambda c: (c, 0) if wt else (0, c)
    side_in = ([row_ssq] if scaled else []) + ([next_gain] if stats else [])

    def specs(tm_, tn_, row, colblk):
        ins = [pl.BlockSpec((tm_, tn_), lambda *g: (row(*g), colblk(*g))) for _ in extras]
        ins += [pl.BlockSpec((tm_, 1), lambda *g: (row(*g), 0))] if scaled else []
        ins += [pl.BlockSpec((1, tn_), lambda *g: (0, colblk(*g)))] if stats else []
        outs = [pl.BlockSpec((tm_, tn_), lambda *g: (row(*g), colblk(*g)))]
        shapes = [jax.ShapeDtypeStruct((m, n_out), out_dtype)]
        blocks = [((tm_, tn_), F32)] * len(extras) + [((tm_, tn_), out_dtype)]
        if stats:
            outs += [pl.BlockSpec((tm_, tn_), lambda *g: (row(*g), colblk(*g))),
                     pl.BlockSpec((tm_, 1), lambda *g: (row(*g), 0))]
            shapes += [jax.ShapeDtypeStruct((m, n_out), BF16), jax.ShapeDtypeStruct((m, 1), F32)]
            blocks += [((tm_, tn_), BF16), ((tm_, LANES), F32)]
        blocks += [((tm_, LANES), F32)] if scaled else []
        return ins, outs, shapes, blocks

    for e in extras:
        assert e.shape[1] == n_out
    ins, outs, shapes, blocks = specs(head_tm, head_tn, lambda j: 0, lambda j: j)
    in_specs = [pl.BlockSpec((head_tm, a.shape[1]), lambda j: (0, 0), pipeline_mode=pl.Buffered(1))
                for a in acts]
    in_specs += [pl.BlockSpec((None,) + w_blk(k, head_tn),
                              lambda j, off=c0 // head_tn: (layer,) + w_idx(off + j))
                 for k, (_, _, c0) in zip(ks, weights)]
    outs += [pl.BlockSpec(w_blk(k, head_tn), lambda j: w_idx(j)) for k in ks]
    shapes += [jax.ShapeDtypeStruct(w_blk(k, n_out), BF16) for k in ks]
    blocks += [((k, head_tn), F32) for k in ks] + [((k, head_tn), BF16) for k in ks]
    res = pl.pallas_call(
        functools.partial(kern, cast=True, n_alias=0, col_axis=0),
        grid=(n_out // head_tn,),
        in_specs=in_specs + ins, out_specs=outs, out_shape=shapes,
        compiler_params=_params(("arbitrary",),
                                _vmem_limit(blocks, single=[((head_tm, a.shape[1]), BF16) for a in acts],
                                            temps=[((head_tm, head_tn), F32)] * n_temps)),
        name=name + "_head",
    )(*acts, *[w for w, _, _ in weights], *extras, *side_in)
    n_res = 3 if stats else 1
    prior, wbfs = res[:n_res], res[n_res:]

    r0 = head_tm // tm
    ins, outs, shapes, blocks = specs(tm, tn, lambda i, j: i + r0, lambda i, j: j)
    in_specs = [pl.BlockSpec((tm, a.shape[1]), lambda i, j: (i + r0, 0)) for a in acts]
    in_specs += [pl.BlockSpec(w_blk(k, tn), lambda i, j: w_idx(j)) for k in ks]
    n_in = len(in_specs) + len(ins)
    blocks += [((tm, a.shape[1]), BF16) for a in acts] + [((k, tn), BF16) for k in ks]
    res = pl.pallas_call(
        functools.partial(kern, cast=False, n_alias=n_res, col_axis=1),
        grid=(m // tm - r0, n_out // tn),
        in_specs=in_specs + ins + [pl.BlockSpec(memory_space=pl.ANY)] * n_res,
        out_specs=outs, out_shape=shapes,
        input_output_aliases={n_in + r: r for r in range(n_res)},
        compiler_params=_params(("parallel", "arbitrary"),
                                _vmem_limit(blocks, temps=[((tm, tn), F32)] * n_temps)),
        name=name + "_tail",
    )(*acts, *wbfs, *extras, *side_in, *prior)
    return tuple(res) if stats else res[0]


def _swiglu_ffn(layer, hb, ssq, w_gate, w_up, w_down, h, next_gain):
    def gate_up(d, e, rs):
        g = d[0] * rs
        return g * jax.nn.sigmoid(g) * (d[1] * rs)

    act = _fused_linear("ffn_gateup", layer, [hb], [(w_gate, 0, 0), (w_up, 0, 0)], [], gate_up,
                        BF16, D_FF, tm=2048, tn=256, head_tm=2048, head_tn=256, n_temps=4,
                        row_ssq=ssq, row_chunks=8)
    return _fused_linear("ffn_down", layer, [act], [(w_down, 0, 0)], [h],
                         lambda d, e, rs: e[0] + 0.5 * d[0], F32, D_MODEL,
                         tm=512, tn=512, head_tm=512, head_tn=256, n_temps=2, next_gain=next_gain,
                         row_chunks=4)


def _log_sigmoid(x):
    return jnp.minimum(x, 0.0) - jnp.log1p(jnp.exp(-jnp.abs(x)))


def _forget_kernel(u_ref, ssq_ref, w_ref, b_ref, c_ref, carry_ref, *, ts):
    @pl.when(pl.program_id(1) == 0)
    def _():
        carry_ref[...] = jnp.zeros_like(carry_ref)

    w = w_ref[...].astype(BF16)
    w = jnp.concatenate([w, jnp.zeros((LANES - N_HEADS, w.shape[1]), BF16)], axis=0)
    logit = lax.dot_general(u_ref[0], w, _NT, preferred_element_type=F32)
    logit = logit * _row_scale(ssq_ref[0]) + b_ref[...]
    c = _log_sigmoid(logit)
    row = lax.broadcasted_iota(jnp.int32, c.shape, 0)
    shift = 1
    while shift < ts:
        c = c + jnp.where(row >= shift, pltpu.roll(c, shift, axis=0), 0.0)
        shift *= 2
    c = c + carry_ref[...]
    c_ref[0] = c
    carry_ref[...] = c[ts - 1:ts, :]


def _forget_cumsum(layer, u3, ssq3, w_in_t, bias, ts=512):
    b, s, d = u3.shape
    blocks = [((1, ts, d), BF16), ((1, ts, LANES), F32), ((N_HEADS, d), F32), ((1, LANES), F32),
              ((1, ts, LANES), F32)]
    return pl.pallas_call(
        functools.partial(_forget_kernel, ts=ts),
        grid=(b, s // ts),
        in_specs=[pl.BlockSpec((1, ts, d), lambda bi, si: (bi, si, 0)),
                  pl.BlockSpec((1, ts, 1), lambda bi, si: (bi, si, 0)),
                  pl.BlockSpec((None, N_HEADS, d), lambda bi, si: (layer, N_QKV // N_HEADS, 0)),
                  pl.BlockSpec((1, LANES), lambda bi, si: (0, 0))],
        out_specs=pl.BlockSpec((1, ts, LANES), lambda bi, si: (bi, si, 0)),
        out_shape=jax.ShapeDtypeStruct((b, s, LANES), F32),
        scratch_shapes=[pltpu.VMEM((1, LANES), F32)],
        compiler_params=_params(("parallel", "arbitrary"),
                                _vmem_limit(blocks, temps=[((ts, LANES), F32)] * 8 + [((LANES, d), BF16)])),
        name="forget_cumsum",
    )(u3, ssq3, w_in_t, bias)


_HPS = 2
_HW = _HPS * HEAD_DIM


def _head_cols(hh):
    return slice(hh * HEAD_DIM, (hh + 1) * HEAD_DIM)


def _fox_kernel(q_ref, k_ref, v_ref, c_ref, ct_ref, o_ref, *, tq):
    hp = pl.program_id(1)
    i = pl.program_id(2)
    q_start = pl.multiple_of(i * tq, tq)
    lane = lax.broadcasted_iota(jnp.int32, (tq, LANES), 1)
    q = [(q_ref[0, :, _head_cols(hh)].astype(F32) * Q_SCALE).astype(BF16)
         for hh in range(_HPS)]
    c_t = [ct_ref[0, hh, :, pl.ds(q_start, tq)] * LOG2E for hh in range(_HPS)]

    def block(j, carry, hh, masked):
        m, l, acc = carry
        start = pl.multiple_of(j * tq, tq)
        k = k_ref[0, pl.ds(start, tq), _head_cols(hh)]
        v = v_ref[0, pl.ds(start, tq), _head_cols(hh)]
        c_s = jnp.sum(jnp.where(lane == hp * _HPS + hh, c_ref[0, pl.ds(start, tq), :], 0.0),
                      axis=1, keepdims=True)
        x = lax.dot_general(k, q[hh], _NT, preferred_element_type=F32) - c_s * LOG2E
        if masked:
            key = lax.broadcasted_iota(jnp.int32, (tq, tq), 0)
            qry = lax.broadcasted_iota(jnp.int32, (tq, tq), 1)
            x = jnp.where(key <= qry, x, -jnp.inf)
        m_new = jnp.maximum(m, jnp.max(x, axis=0, keepdims=True) + c_t[hh])
        p = jnp.exp2(x + (c_t[hh] - m_new))
        alpha = jnp.exp2(m - m_new)
        l = alpha * l + jnp.sum(p, axis=0, keepdims=True)
        acc = alpha * acc + lax.dot_general(v, p.astype(BF16), _TN, preferred_element_type=F32)
        return m_new, l, acc

    def blocks(j, carries, masked):
        return tuple(block(j, carries[hh], hh, masked) for hh in range(_HPS))

    init = (jnp.full((1, tq), -jnp.inf, F32), jnp.zeros((1, tq), F32),
            jnp.zeros((HEAD_DIM, tq), F32))
    carries = lax.fori_loop(0, i, lambda j, c: blocks(j, c, False), (init,) * _HPS)
    carries = blocks(i, carries, True)
    for hh, (_, l, acc) in enumerate(carries):
        o_ref[0, :, _head_cols(hh)] = (acc / l).T.astype(o_ref.dtype)


def _fox_attention(z3, c, ct, tq=512):
    b, s, _ = z3.shape
    npair = N_HEADS // _HPS
    blocks = [((1, tq, _HW), BF16), ((1, s, _HW), BF16), ((1, s, _HW), BF16),
              ((1, s, LANES), F32), ((1, _HPS, 1, s), F32), ((1, tq, _HW), BF16)]
    return pl.pallas_call(
        functools.partial(_fox_kernel, tq=tq),
        grid=(b, npair, s // tq),
        in_specs=[pl.BlockSpec((1, tq, _HW), lambda bi, hp, i: (bi, i, hp)),
                  pl.BlockSpec((1, s, _HW), lambda bi, hp, i: (bi, 0, npair + hp)),
                  pl.BlockSpec((1, s, _HW), lambda bi, hp, i: (bi, 0, 2 * npair + hp)),
                  pl.BlockSpec((1, s, LANES), lambda bi, hp, i: (bi, 0, 0)),
                  pl.BlockSpec((1, _HPS, 1, s), lambda bi, hp, i: (bi, hp, 0, 0))],
        out_specs=pl.BlockSpec((1, tq, _HW), lambda bi, hp, i: (bi, i, hp)),
        out_shape=jax.ShapeDtypeStruct((b, s, WIDTH), BF16),
        compiler_params=_params(("parallel", "parallel", "arbitrary"),
                                _vmem_limit(blocks, temps=[((tq, tq), F32)] * 6 * _HPS)),
        name="fox_attention",
    )(z3, z3, z3, c, ct)


_QB = 8 * CHUNK
_N_BIAS_VEC = 12


def _bias_pieces_index():
    e = 128 * (np.arange(_N_BIAS_VEC)[:, None] - 4) + np.arange(LANES)[None, :]
    dist = _QB - e
    return np.clip(dist, -REL_CLIP, REL_CLIP) + REL_CLIP


def _chunk_kernel(q_ref, kp_ref, kc_ref, vp_ref, vc_ref, g_ref, o_ref, bias_ref):
    bi = pl.program_id(1)
    i = pl.program_id(2)

    @pl.when((bi == 0) & (i == 0))
    def _():
        r = lax.broadcasted_iota(jnp.int32, (LANES, LANES), 0)
        col = lax.broadcasted_iota(jnp.int32, (LANES, LANES), 1)
        upper = col >= r
        for hh in range(_HPS):
            rolled = [pltpu.roll(jnp.broadcast_to(g_ref[hh, k:k + 1, :] * LOG2E, (LANES, LANES)),
                                 0, 1, stride=1, stride_axis=0) for k in range(_N_BIAS_VEC)]
            for rb in range(_QB // LANES):
                for cb in range(2 * _QB // LANES):
                    delta = cb - rb + 4
                    tile = jnp.where(upper, rolled[delta], rolled[delta - 1])
                    q_chunk = (rb * LANES + r) // CHUNK
                    k_chunk = (cb * LANES + col) // CHUNK
                    ok = (k_chunk >= q_chunk) & (k_chunk <= q_chunk + LEFT_CHUNKS)
                    bias_ref[hh, cb * LANES:(cb + 1) * LANES, rb * LANES:(rb + 1) * LANES] = (
                        jnp.where(ok, tile, -jnp.inf).T)

    for hh in range(_HPS):
        cols = _head_cols(hh)
        q = (q_ref[0, :, cols].astype(F32) * Q_SCALE).astype(BF16)
        x_prev = (lax.dot_general(kp_ref[0, :, cols], q, _NT, preferred_element_type=F32)
                  + bias_ref[hh, :_QB, :])
        x_prev = jnp.where(i > 0, x_prev, -jnp.inf)
        x_cur = (lax.dot_general(kc_ref[0, :, cols], q, _NT, preferred_element_type=F32)
                 + bias_ref[hh, _QB:, :])
        m = jnp.maximum(jnp.max(x_prev, axis=0, keepdims=True),
                        jnp.max(x_cur, axis=0, keepdims=True))
        p_prev = jnp.exp2(x_prev - m)
        p_cur = jnp.exp2(x_cur - m)
        l = jnp.sum(p_prev, axis=0, keepdims=True) + jnp.sum(p_cur, axis=0, keepdims=True)
        acc = lax.dot_general(vp_ref[0, :, cols], p_prev.astype(BF16), _TN,
                              preferred_element_type=F32)
        acc = acc + lax.dot_general(vc_ref[0, :, cols], p_cur.astype(BF16), _TN,
                                    preferred_element_type=F32)
        o_ref[0, :, cols] = (acc / l).T.astype(o_ref.dtype)


def _chunk_attention(z3, bias_pieces):
    b, s, _ = z3.shape
    nb = s // _QB
    npair = N_HEADS // _HPS
    qo, ko, vo = 3 * npair, 4 * npair, 5 * npair
    blk = (1, _QB, _HW)
    blocks = [(blk, BF16)] * 6 + [((_HPS, 16, LANES), F32)]
    prev = lambda i: jnp.maximum(i - 1, 0)
    return pl.pallas_call(
        _chunk_kernel,
        grid=(npair, b, nb),
        in_specs=[pl.BlockSpec(blk, lambda hp, bi, i: (bi, i, qo + hp)),
                  pl.BlockSpec(blk, lambda hp, bi, i: (bi, prev(i), ko + hp)),
                  pl.BlockSpec(blk, lambda hp, bi, i: (bi, i, ko + hp)),
                  pl.BlockSpec(blk, lambda hp, bi, i: (bi, prev(i), vo + hp)),
                  pl.BlockSpec(blk, lambda hp, bi, i: (bi, i, vo + hp)),
                  pl.BlockSpec((_HPS, 16, LANES), lambda hp, bi, i: (hp, 0, 0))],
        out_specs=pl.BlockSpec(blk, lambda hp, bi, i: (bi, i, hp)),
        out_shape=jax.ShapeDtypeStruct((b, s, WIDTH), BF16),
        scratch_shapes=[pltpu.VMEM((_HPS, 2 * _QB, _QB), F32)],
        compiler_params=_params(("arbitrary", "arbitrary", "arbitrary"),
                                _vmem_limit(blocks, single=[((_HPS, 2 * _QB, _QB), F32)],
                                            temps=[((_QB, _QB), F32)] * 8 * _HPS)),
        name="chunk_attention",
    )(z3, z3, z3, z3, z3, bias_pieces)


def kernel(x, p, ffn1_norm, ffn1_w_gate, ffn1_w_up, ffn1_w_down, mix_norm, w_in, fox_forget_bias,
           rel_bias, w_branch_gate, w_proj_a, w_proj_b, w_out, ffn2_norm, ffn2_w_gate, ffn2_w_up,
           ffn2_w_down, ple_norm, ple_w_gate, ple_w_proj, final_norm):
    forget_bias = jnp.pad(fox_forget_bias, ((0, 0), (0, LANES - N_HEADS)))[:, None, :]
    p_bf = p.astype(BF16).reshape(DEPTH, M_TOK, PLE_DIM)
    bias_pieces = jnp.pad(rel_bias[:, :, _bias_pieces_index()],
                          ((0, 0), (0, 0), (0, 16 - _N_BIAS_VEC), (0, 0)))
    sig = jax.nn.sigmoid
    gain = lambda g: g.reshape(1, D_MODEL)
    w_in_t = jnp.swapaxes(w_in, 1, 2)

    h = x.reshape(M_TOK, D_MODEL)
    hb, ssq = _prenorm(h, ffn1_norm[0])
    for i in range(DEPTH):
        h, hb, ssq = _swiglu_ffn(i, hb, ssq, ffn1_w_gate, ffn1_w_up, ffn1_w_down, h, gain(mix_norm[i]))

        z = _fused_linear("qkv_proj", i, [hb], [(w_in_t, 0, 0)], [], lambda d, e, rs: d[0] * rs,
                          BF16, N_QKV, tm=1024, tn=1024, head_tm=1024, head_tn=512, n_temps=2,
                          row_ssq=ssq, w_rows_are_outputs=True, row_chunks=4)
        z3 = z.reshape(BATCH, SEQ, N_QKV)
        c = _forget_cumsum(i, hb.reshape(BATCH, SEQ, D_MODEL), ssq.reshape(BATCH, SEQ, 1), w_in_t,
                           forget_bias[i])
        ct = c[:, :, :N_HEADS].transpose(0, 2, 1)[:, :, None, :]
        attn_a = _fox_attention(z3, c, ct).reshape(M_TOK, WIDTH)
        attn_b = _chunk_attention(z3, bias_pieces[i]).reshape(M_TOK, WIDTH)
        mix = _fused_linear(
            "branch_mix", i, [hb, attn_a, attn_b],
            [(w_branch_gate, 0, 0), (w_branch_gate, 0, D_MODEL), (w_proj_a, 1, 0), (w_proj_b, 2, 0)],
            [], lambda d, e, rs: sig(d[0] * rs) * d[2] + sig(d[1] * rs) * d[3], BF16, D_MODEL,
            tm=512, tn=512, head_tm=512, head_tn=256, n_temps=6, row_ssq=ssq, row_chunks=2)
        h, hb, ssq = _fused_linear("out_proj", i, [mix], [(w_out, 0, 0)], [h],
                                   lambda d, e, rs: e[0] + d[0], F32, D_MODEL,
                                   tm=1024, tn=512, head_tm=1024, head_tn=512, n_temps=2,
                                   next_gain=gain(ffn2_norm[i]), row_chunks=4)

        h, hb, ssq = _swiglu_ffn(i, hb, ssq, ffn2_w_gate, ffn2_w_up, ffn2_w_down, h, gain(ple_norm[i]))

        last = i + 1 == DEPTH
        res = _fused_linear("ple", i, [hb, p_bf[i]], [(ple_w_gate, 0, 0), (ple_w_proj, 1, 0)], [h],
                            lambda d, e, rs: e[0] + sig(d[0] * rs) * d[1], F32, D_MODEL,
                            tm=1024, tn=512, head_tm=1024, head_tn=512, n_temps=4, row_ssq=ssq,
                            next_gain=None if last else gain(ffn1_norm[i + 1]), row_chunks=4)
        h, hb, ssq = (res, None, None) if last else res
    out = _rmsnorm(h, final_norm, F32)
    return out.reshape(BATCH, SEQ, D_MODEL)
```

```python
import functools
import math

import jax
import jax.numpy as jnp
import numpy as np
from jax import lax
from jax.experimental import pallas as pl
from jax.experimental.pallas import tpu as pltpu

D_MODEL = 4096
BATCH = 4
SEQ = 2048
DEPTH = 2
CHUNK = 64
PLE_DIM = 256
D_FF = 11008
HEAD_DIM = 128
N_HEADS = D_MODEL // (2 * HEAD_DIM)
WIDTH = N_HEADS * HEAD_DIM
LEFT_CHUNKS = 8
REL_CLIP = 128
RMS_EPS = 1e-6
N_QKV = 6 * WIDTH
M_TOK = BATCH * SEQ
LOG2E = math.log2(math.e)
Q_SCALE = HEAD_DIM ** -0.5 * LOG2E

F32 = jnp.float32
BF16 = jnp.bfloat16

LANES = 128
V7X_VMEM_REQUEST_CAP = 60000 * 1024
MIB = 1024 * 1024

_NT = (((1,), (1,)), ((), ()))
_TN = (((0,), (0,)), ((), ()))


def _nbytes(shape, dtype):
    return int(np.prod([s for s in shape if s is not None])) * jnp.dtype(dtype).itemsize


def _vmem_limit(blocks, single=(), temps=()):
    total = 2 * sum(_nbytes(s, d) for s, d in blocks)
    total += sum(_nbytes(s, d) for s, d in single)
    total += sum(_nbytes(s, d) for s, d in temps)
    total += 4 * MIB
    return min(total, V7X_VMEM_REQUEST_CAP)


def _params(semantics, limit):
    return pltpu.CompilerParams(dimension_semantics=semantics, vmem_limit_bytes=limit)


def _rmsnorm_kernel(h_ref, g_ref, o_ref):
    x = h_ref[...]
    ms = jnp.mean(x * x, axis=-1, keepdims=True)
    o_ref[...] = (x * lax.rsqrt(ms + RMS_EPS) * g_ref[...]).astype(o_ref.dtype)


def _rmsnorm(h, g, out_dtype, tr=512):
    m, d = h.shape
    blocks = [((tr, d), F32), ((1, d), F32), ((tr, d), out_dtype)]
    return pl.pallas_call(
        _rmsnorm_kernel,
        grid=(m // tr,),
        in_specs=[pl.BlockSpec((tr, d), lambda i: (i, 0)),
                  pl.BlockSpec((1, d), lambda i: (0, 0))],
        out_specs=pl.BlockSpec((tr, d), lambda i: (i, 0)),
        out_shape=jax.ShapeDtypeStruct((m, d), out_dtype),
        compiler_params=_params(("parallel",), _vmem_limit(blocks, temps=[((tr, d), F32)] * 2)),
        name="rmsnorm",
    )(h, g.reshape(1, d))


def _row_scale(ssq):
    return lax.rsqrt(ssq * (1.0 / D_MODEL) + RMS_EPS)


def _prenorm_kernel(h_ref, g_ref, h_out_ref, hb_ref, ssq_ref, spare_ref):
    x = h_ref[...]
    h_out_ref[...] = x
    hb_ref[...] = (x * g_ref[...]).astype(BF16)
    ssq = jnp.sum(x * x, axis=-1, keepdims=True)
    ssq_ref[...] = ssq
    spare_ref[...] = ssq


def _prenorm(h, g, tr=512):
    m, d = h.shape
    blocks = [((tr, d), F32), ((1, d), F32), ((tr, d), F32), ((tr, d), BF16)] + [((tr, LANES), F32)] * 2
    return pl.pallas_call(
        _prenorm_kernel,
        grid=(m // tr,),
        in_specs=[pl.BlockSpec((tr, d), lambda i: (i, 0)),
                  pl.BlockSpec((1, d), lambda i: (0, 0))],
        out_specs=[pl.BlockSpec((tr, d), lambda i: (i, 0)),
                   pl.BlockSpec((tr, d), lambda i: (i, 0)),
                   pl.BlockSpec((tr, 1), lambda i: (i, 0)),
                   pl.BlockSpec((tr, 1), lambda i: (i, 0))],
        out_shape=[jax.ShapeDtypeStruct((m, d), F32), jax.ShapeDtypeStruct((m, d), BF16),
                   jax.ShapeDtypeStruct((m, 1), F32), jax.ShapeDtypeStruct((m, 1), F32)],
        compiler_params=_params(("parallel",), _vmem_limit(blocks, temps=[((tr, d), F32)] * 2)),
        name="prenorm",
    )(h, g.reshape(1, d))


def _linear_kernel(*refs, w_act, n_acts, n_extras, epilogue, cast, scaled, stats, n_alias, col_axis,
                   w_rows_are_outputs, row_chunks, tile_rows):
    nw = len(w_act)
    it = iter(refs)
    take = lambda n: [next(it) for _ in range(n)]
    acts, ws, extras = take(n_acts), take(nw), take(n_extras)
    ssq_in = take(1)[0] if scaled else None
    gain = take(1)[0] if stats else None
    take(n_alias)
    out = take(1)[0]
    hb, ssq_out = take(2) if stats else (None, None)
    if cast:
        wbfs = take(nw)
        for w, wb in zip(ws, wbfs):
            wb[...] = w[...].astype(BF16)
        ws = wbfs
    dims = _NT if w_rows_are_outputs else (((1,), (0,)), ((), ()))
    rc = tile_rows // row_chunks
    fill = (out.shape[0] - tile_rows) // row_chunks
    parts = []
    for c in range(row_chunks):
        rows = slice(c * rc, (c + 1) * rc)
        if fill:
            out[tile_rows + c * fill:tile_rows + (c + 1) * fill, :] = jnp.zeros(
                (fill, out.shape[1]), out.dtype)
        dots = [lax.dot_general(acts[ai][rows, :], w[...], dims, preferred_element_type=F32)
                for ai, w in zip(w_act, ws)]
        rs = _row_scale(ssq_in[rows, :]) if scaled else None
        res = epilogue(dots, [e[rows, :] for e in extras], rs)
        out[rows, :] = res.astype(out.dtype)
        if stats:
            hb[rows, :] = (res * gain[...]).astype(BF16)
            parts.append(jnp.sum(res * res, axis=1, keepdims=True))
    if stats:
        part = jnp.concatenate(parts, axis=0) if row_chunks > 1 else parts[0]
        col = pl.program_id(col_axis)

        @pl.when(col == 0)
        def _():
            ssq_out[...] = part

        @pl.when(col != 0)
        def _():
            ssq_out[...] += part


def _fused_linear(name, layer, acts, weights, extras, epilogue, out_dtype, n_out, *,
                  tm, tn, head_tm, head_tn, n_temps, row_ssq=None, next_gain=None, stats_bufs=None,
                  w_rows_are_outputs=False, row_chunks=1):
    m = acts[0].shape[0]
    w_act = tuple(ai for _, ai, _ in weights)
    wt = w_rows_are_outputs
    ks = [w.shape[2 if wt else 1] for w, _, _ in weights]
    scaled, stats = row_ssq is not None, next_gain is not None
    in_place = len(extras) == 1
    assert len(extras) <= 1 and (in_place or not stats) and stats == (stats_bufs is not None)
    kern = functools.partial(_linear_kernel, w_act=w_act, n_acts=len(acts), n_extras=len(extras),
                             epilogue=epilogue, scaled=scaled, stats=stats, w_rows_are_outputs=wt,
                             row_chunks=row_chunks)
    w_blk = lambda k, n: (n, k) if wt else (k, n)
    w_idx = lambda c: (c, 0) if wt else (0, c)
    side_in = ([row_ssq] if scaled else []) + ([next_gain] if stats else [])

    def specs(tm_, tn_, row, colblk):
        ins = [pl.BlockSpec((tm_, tn_), lambda *g: (row(*g), colblk(*g))) for _ in extras]
        ins += [pl.BlockSpec((tm_, 1), lambda *g: (row(*g), 0))] if scaled else []
        ins += [pl.BlockSpec((1, tn_), lambda *g: (0, colblk(*g)))] if stats else []
        outs = [pl.BlockSpec((tm_, tn_), lambda *g: (row(*g), colblk(*g)))]
        shapes = [jax.ShapeDtypeStruct((m, n_out), out_dtype)]
        blocks = [((tm_, tn_), F32)] * len(extras) + [((tm_, tn_), out_dtype)]
        if stats:
            outs += [pl.BlockSpec((tm_, tn_), lambda *g: (row(*g), colblk(*g))),
                     pl.BlockSpec((tm_, 1), lambda *g: (row(*g), 0))]
            shapes += [jax.ShapeDtypeStruct((m, n_out), BF16), jax.ShapeDtypeStruct((m, 1), F32)]
            blocks += [((tm_, tn_), BF16), ((tm_, LANES), F32)]
        blocks += [((tm_, LANES), F32)] if scaled else []
        return ins, outs, shapes, blocks

    for e in extras:
        assert e.shape[1] == n_out
    ins, outs, shapes, blocks = specs(head_tm, head_tn, lambda j: 0, lambda j: j)
    in_specs = [pl.BlockSpec((head_tm, a.shape[1]), lambda j: (0, 0), pipeline_mode=pl.Buffered(1))
                for a in acts]
    in_specs += [pl.BlockSpec((None,) + w_blk(k, head_tn),
                              lambda j, off=c0 // head_tn: (layer,) + w_idx(off + j))
                 for k, (_, _, c0) in zip(ks, weights)]
    n_in = len(in_specs) + len(ins)
    if in_place:
        aliases = {len(in_specs): 0}
        aliases.update({n_in + r: 1 + r for r in range(2 if stats else 0)})
    else:
        aliases = {}
        outs[0] = pl.BlockSpec((m, head_tn), lambda j: (0, j))
        blocks += [((m - head_tm, head_tn), out_dtype)]
    outs += [pl.BlockSpec(w_blk(k, head_tn), lambda j: w_idx(j)) for k in ks]
    shapes += [jax.ShapeDtypeStruct(w_blk(k, n_out), BF16) for k in ks]
    blocks += [((k, head_tn), F32) for k in ks] + [((k, head_tn), BF16) for k in ks]
    bufs = list(stats_bufs) if stats else []
    res = pl.pallas_call(
        functools.partial(kern, cast=True, n_alias=len(bufs), col_axis=0, tile_rows=head_tm),
        grid=(n_out // head_tn,),
        in_specs=in_specs + ins + [pl.BlockSpec(memory_space=pl.ANY)] * len(bufs),
        out_specs=outs, out_shape=shapes, input_output_aliases=aliases,
        compiler_params=_params(("arbitrary",),
                                _vmem_limit(blocks, single=[((head_tm, a.shape[1]), BF16) for a in acts],
                                            temps=[((head_tm, head_tn), F32)] * n_temps)),
        name=name + "_head",
    )(*acts, *[w for w, _, _ in weights], *extras, *side_in, *bufs)
    n_res = 3 if stats else 1
    prior, wbfs = res[:n_res], res[n_res:]

    r0 = head_tm // tm
    ins, outs, shapes, blocks = specs(tm, tn, lambda i, j: i + r0, lambda i, j: j)
    in_specs = [pl.BlockSpec((tm, a.shape[1]), lambda i, j: (i + r0, 0)) for a in acts]
    in_specs += [pl.BlockSpec(w_blk(k, tn), lambda i, j: w_idx(j)) for k in ks]
    n_in = len(in_specs) + len(ins)
    blocks += [((tm, a.shape[1]), BF16) for a in acts] + [((k, tn), BF16) for k in ks]
    if in_place:
        residual, unread = [prior[0]], list(prior[1:])
        aliases = {len(in_specs): 0}
        aliases.update({n_in + r: 1 + r for r in range(len(unread))})
    else:
        residual, unread = [], list(prior)
        aliases = {n_in: 0}
    res = pl.pallas_call(
        functools.partial(kern, cast=False, n_alias=len(unread), col_axis=1, tile_rows=tm),
        grid=(m // tm - r0, n_out // tn),
        in_specs=in_specs + ins + [pl.BlockSpec(memory_space=pl.ANY)] * len(unread),
        out_specs=outs, out_shape=shapes, input_output_aliases=aliases,
        compiler_params=_params(("parallel", "arbitrary"),
                                _vmem_limit(blocks, temps=[((tm, tn), F32)] * n_temps)),
        name=name + "_tail",
    )(*acts, *wbfs, *residual, *side_in, *unread)
    return tuple(res) if stats else res[0]


def _swiglu_ffn(layer, hb, ssq, w_gate, w_up, w_down, h, next_gain):
    def gate_up(d, e, rs):
        g = d[0] * rs
        return g * jax.nn.sigmoid(g) * (d[1] * rs)

    act = _fused_linear("ffn_gateup", layer, [hb], [(w_gate, 0, 0), (w_up, 0, 0)], [], gate_up,
                        BF16, D_FF, tm=2048, tn=256, head_tm=2048, head_tn=256, n_temps=4,
                        row_ssq=ssq, row_chunks=8)
    return _fused_linear("ffn_down", layer, [act], [(w_down, 0, 0)], [h],
                         lambda d, e, rs: e[0] + 0.5 * d[0], F32, D_MODEL,
                         tm=512, tn=512, head_tm=512, head_tn=256, n_temps=2, next_gain=next_gain,
                         stats_bufs=(hb, ssq), row_chunks=4)


def _log_sigmoid(x):
    return jnp.minimum(x, 0.0) - jnp.log1p(jnp.exp(-jnp.abs(x)))


def _forget_kernel(u_ref, ssq_ref, w_ref, b_ref, c_ref, carry_ref, *, ts):
    @pl.when(pl.program_id(1) == 0)
    def _():
        carry_ref[...] = jnp.zeros_like(carry_ref)

    w = w_ref[...].astype(BF16)
    w = jnp.concatenate([w, jnp.zeros((LANES - N_HEADS, w.shape[1]), BF16)], axis=0)
    logit = lax.dot_general(u_ref[0], w, _NT, preferred_element_type=F32)
    logit = logit * _row_scale(ssq_ref[0]) + b_ref[...]
    c = _log_sigmoid(logit)
    row = lax.broadcasted_iota(jnp.int32, c.shape, 0)
    shift = 1
    while shift < ts:
        c = c + jnp.where(row >= shift, pltpu.roll(c, shift, axis=0), 0.0)
        shift *= 2
    c = c + carry_ref[...]
    c_ref[0] = c
    carry_ref[...] = c[ts - 1:ts, :]


def _forget_cumsum(layer, u3, ssq3, w_in_t, bias, ts=512):
    b, s, d = u3.shape
    blocks = [((1, ts, d), BF16), ((1, ts, LANES), F32), ((N_HEADS, d), F32), ((1, LANES), F32),
              ((1, ts, LANES), F32)]
    return pl.pallas_call(
        functools.partial(_forget_kernel, ts=ts),
        grid=(b, s // ts),
        in_specs=[pl.BlockSpec((1, ts, d), lambda bi, si: (bi, si, 0)),
                  pl.BlockSpec((1, ts, 1), lambda bi, si: (bi, si, 0)),
                  pl.BlockSpec((None, N_HEADS, d), lambda bi, si: (layer, N_QKV // N_HEADS, 0)),
                  pl.BlockSpec((1, LANES), lambda bi, si: (0, 0))],
        out_specs=pl.BlockSpec((1, ts, LANES), lambda bi, si: (bi, si, 0)),
        out_shape=jax.ShapeDtypeStruct((b, s, LANES), F32),
        scratch_shapes=[pltpu.VMEM((1, LANES), F32)],
        compiler_params=_params(("parallel", "arbitrary"),
                                _vmem_limit(blocks, temps=[((ts, LANES), F32)] * 8 + [((LANES, d), BF16)])),
        name="forget_cumsum",
    )(u3, ssq3, w_in_t, bias)


_HPS = 2
_HW = _HPS * HEAD_DIM


def _head_cols(hh):
    return slice(hh * HEAD_DIM, (hh + 1) * HEAD_DIM)


def _fox_kernel(q_ref, k_ref, v_ref, c_ref, ct_ref, o_ref, *, tq):
    hp = pl.program_id(1)
    i = pl.program_id(2)
    q_start = pl.multiple_of(i * tq, tq)
    lane = lax.broadcasted_iota(jnp.int32, (tq, LANES), 1)
    q = [(q_ref[0, :, _head_cols(hh)].astype(F32) * Q_SCALE).astype(BF16)
         for hh in range(_HPS)]
    c_t = [ct_ref[0, hh, :, pl.ds(q_start, tq)] * LOG2E for hh in range(_HPS)]

    def block(j, carry, hh, masked):
        m, l, acc = carry
        start = pl.multiple_of(j * tq, tq)
        k = k_ref[0, pl.ds(start, tq), _head_cols(hh)]
        v = v_ref[0, pl.ds(start, tq), _head_cols(hh)]
        c_s = jnp.sum(jnp.where(lane == hp * _HPS + hh, c_ref[0, pl.ds(start, tq), :], 0.0),
                      axis=1, keepdims=True)
        x = lax.dot_general(k, q[hh], _NT, preferred_element_type=F32) - c_s * LOG2E
        if masked:
            key = lax.broadcasted_iota(jnp.int32, (tq, tq), 0)
            qry = lax.broadcasted_iota(jnp.int32, (tq, tq), 1)
            x = jnp.where(key <= qry, x, -jnp.inf)
        m_new = jnp.maximum(m, jnp.max(x, axis=0, keepdims=True) + c_t[hh])
        p = jnp.exp2(x + (c_t[hh] - m_new))
        alpha = jnp.exp2(m - m_new)
        l = alpha * l + jnp.sum(p, axis=0, keepdims=True)
        acc = alpha * acc + lax.dot_general(v, p.astype(BF16), _TN, preferred_element_type=F32)
        return m_new, l, acc

    def blocks(j, carries, masked):
        return tuple(block(j, carries[hh], hh, masked) for hh in range(_HPS))

    init = (jnp.full((1, tq), -jnp.inf, F32), jnp.zeros((1, tq), F32),
            jnp.zeros((HEAD_DIM, tq), F32))
    carries = lax.fori_loop(0, i, lambda j, c: blocks(j, c, False), (init,) * _HPS)
    carries = blocks(i, carries, True)
    for hh, (_, l, acc) in enumerate(carries):
        o_ref[0, :, _head_cols(hh)] = (acc / l).T.astype(o_ref.dtype)


def _fox_attention(z3, c, ct, tq=512):
    b, s, _ = z3.shape
    npair = N_HEADS // _HPS
    blocks = [((1, tq, _HW), BF16), ((1, s, _HW), BF16), ((1, s, _HW), BF16),
              ((1, s, LANES), F32), ((1, _HPS, 1, s), F32), ((1, tq, _HW), BF16)]
    return pl.pallas_call(
        functools.partial(_fox_kernel, tq=tq),
        grid=(b, npair, s // tq),
        in_specs=[pl.BlockSpec((1, tq, _HW), lambda bi, hp, i: (bi, i, hp)),
                  pl.BlockSpec((1, s, _HW), lambda bi, hp, i: (bi, 0, npair + hp)),
                  pl.BlockSpec((1, s, _HW), lambda bi, hp, i: (bi, 0, 2 * npair + hp)),
                  pl.BlockSpec((1, s, LANES), lambda bi, hp, i: (bi, 0, 0)),
                  pl.BlockSpec((1, _HPS, 1, s), lambda bi, hp, i: (bi, hp, 0, 0))],
        out_specs=pl.BlockSpec((1, tq, _HW), lambda bi, hp, i: (bi, i, hp)),
        out_shape=jax.ShapeDtypeStruct((b, s, WIDTH), BF16),
        compiler_params=_params(("parallel", "parallel", "arbitrary"),
                                _vmem_limit(blocks, temps=[((tq, tq), F32)] * 6 * _HPS)),
        name="fox_attention",
    )(z3, z3, z3, c, ct)


_QB = 8 * CHUNK
_N_BIAS_VEC = 12


def _bias_pieces_index():
    e = 128 * (np.arange(_N_BIAS_VEC)[:, None] - 4) + np.arange(LANES)[None, :]
    dist = _QB - e
    return np.clip(dist, -REL_CLIP, REL_CLIP) + REL_CLIP


def _chunk_kernel(q_ref, kp_ref, kc_ref, vp_ref, vc_ref, g_ref, o_ref, bias_ref):
    bi = pl.program_id(1)
    i = pl.program_id(2)

    @pl.when((bi == 0) & (i == 0))
    def _():
        r = lax.broadcasted_iota(jnp.int32, (LANES, LANES), 0)
        col = lax.broadcasted_iota(jnp.int32, (LANES, LANES), 1)
        upper = col >= r
        for hh in range(_HPS):
            rolled = [pltpu.roll(jnp.broadcast_to(g_ref[hh, k:k + 1, :] * LOG2E, (LANES, LANES)),
                                 0, 1, stride=1, stride_axis=0) for k in range(_N_BIAS_VEC)]
            for rb in range(_QB // LANES):
                for cb in range(2 * _QB // LANES):
                    delta = cb - rb + 4
                    tile = jnp.where(upper, rolled[delta], rolled[delta - 1])
                    q_chunk = (rb * LANES + r) // CHUNK
                    k_chunk = (cb * LANES + col) // CHUNK
                    ok = (k_chunk >= q_chunk) & (k_chunk <= q_chunk + LEFT_CHUNKS)
                    bias_ref[hh, cb * LANES:(cb + 1) * LANES, rb * LANES:(rb + 1) * LANES] = (
                        jnp.where(ok, tile, -jnp.inf).T)

    for hh in range(_HPS):
        cols = _head_cols(hh)
        q = (q_ref[0, :, cols].astype(F32) * Q_SCALE).astype(BF16)
        x_prev = (lax.dot_general(kp_ref[0, :, cols], q, _NT, preferred_element_type=F32)
                  + bias_ref[hh, :_QB, :])
        x_prev = jnp.where(i > 0, x_prev, -jnp.inf)
        x_cur = (lax.dot_general(kc_ref[0, :, cols], q, _NT, preferred_element_type=F32)
                 + bias_ref[hh, _QB:, :])
        m = jnp.maximum(jnp.max(x_prev, axis=0, keepdims=True),
                        jnp.max(x_cur, axis=0, keepdims=True))
        p_prev = jnp.exp2(x_prev - m)
        p_cur = jnp.exp2(x_cur - m)
        l = jnp.sum(p_prev, axis=0, keepdims=True) + jnp.sum(p_cur, axis=0, keepdims=True)
        acc = lax.dot_general(vp_ref[0, :, cols], p_prev.astype(BF16), _TN,
                              preferred_element_type=F32)
        acc = acc + lax.dot_general(vc_ref[0, :, cols], p_cur.astype(BF16), _TN,
                                    preferred_element_type=F32)
        o_ref[0, :, cols] = (acc / l).T.astype(o_ref.dtype)


def _chunk_attention(z3, bias_pieces):
    b, s, _ = z3.shape
    nb = s // _QB
    npair = N_HEADS // _HPS
    qo, ko, vo = 3 * npair, 4 * npair, 5 * npair
    blk = (1, _QB, _HW)
    blocks = [(blk, BF16)] * 6 + [((_HPS, 16, LANES), F32)]
    prev = lambda i: jnp.maximum(i - 1, 0)
    return pl.pallas_call(
        _chunk_kernel,
        grid=(npair, b, nb),
        in_specs=[pl.BlockSpec(blk, lambda hp, bi, i: (bi, i, qo + hp)),
                  pl.BlockSpec(blk, lambda hp, bi, i: (bi, prev(i), ko + hp)),
                  pl.BlockSpec(blk, lambda hp, bi, i: (bi, i, ko + hp)),
                  pl.BlockSpec(blk, lambda hp, bi, i: (bi, prev(i), vo + hp)),
                  pl.BlockSpec(blk, lambda hp, bi, i: (bi, i, vo + hp)),
                  pl.BlockSpec((_HPS, 16, LANES), lambda hp, bi, i: (hp, 0, 0))],
        out_specs=pl.BlockSpec(blk, lambda hp, bi, i: (bi, i, hp)),
        out_shape=jax.ShapeDtypeStruct((b, s, WIDTH), BF16),
        scratch_shapes=[pltpu.VMEM((_HPS, 2 * _QB, _QB), F32)],
        compiler_params=_params(("arbitrary", "arbitrary", "arbitrary"),
                                _vmem_limit(blocks, single=[((_HPS, 2 * _QB, _QB), F32)],
                                            temps=[((_QB, _QB), F32)] * 8 * _HPS)),
        name="chunk_attention",
    )(z3, z3, z3, z3, z3, bias_pieces)


def kernel(x, p, ffn1_norm, ffn1_w_gate, ffn1_w_up, ffn1_w_down, mix_norm, w_in, fox_forget_bias,
           rel_bias, w_branch_gate, w_proj_a, w_proj_b, w_out, ffn2_norm, ffn2_w_gate, ffn2_w_up,
           ffn2_w_down, ple_norm, ple_w_gate, ple_w_proj, final_norm):
    forget_bias = jnp.pad(fox_forget_bias, ((0, 0), (0, LANES - N_HEADS)))[:, None, :]
    p_bf = p.astype(BF16).reshape(DEPTH, M_TOK, PLE_DIM)
    bias_pieces = jnp.pad(rel_bias[:, :, _bias_pieces_index()],
                          ((0, 0), (0, 0), (0, 16 - _N_BIAS_VEC), (0, 0)))
    sig = jax.nn.sigmoid
    gain = lambda g: g.reshape(1, D_MODEL)
    w_in_t = jnp.swapaxes(w_in, 1, 2)

    h, hb, ssq, ssq_spare = _prenorm(x.reshape(M_TOK, D_MODEL), ffn1_norm[0])
    for i in range(DEPTH):
        h, hb, ssq = _swiglu_ffn(i, hb, ssq, ffn1_w_gate, ffn1_w_up, ffn1_w_down, h, gain(mix_norm[i]))

        z = _fused_linear("qkv_proj", i, [hb], [(w_in_t, 0, 0)], [], lambda d, e, rs: d[0] * rs,
                          BF16, N_QKV, tm=1024, tn=1024, head_tm=1024, head_tn=512, n_temps=2,
                          row_ssq=ssq, w_rows_are_outputs=True, row_chunks=4)
        z3 = z.reshape(BATCH, SEQ, N_QKV)
        c = _forget_cumsum(i, hb.reshape(BATCH, SEQ, D_MODEL), ssq.reshape(BATCH, SEQ, 1), w_in_t,
                           forget_bias[i])
        ct = c[:, :, :N_HEADS].transpose(0, 2, 1)[:, :, None, :]
        attn_a = _fox_attention(z3, c, ct).reshape(M_TOK, WIDTH)
        attn_b = _chunk_attention(z3, bias_pieces[i]).reshape(M_TOK, WIDTH)
        mix = _fused_linear(
            "branch_mix", i, [hb, attn_a, attn_b],
            [(w_branch_gate, 0, 0), (w_branch_gate, 0, D_MODEL), (w_proj_a, 1, 0), (w_proj_b, 2, 0)],
            [], lambda d, e, rs: sig(d[0] * rs) * d[2] + sig(d[1] * rs) * d[3], BF16, D_MODEL,
            tm=512, tn=512, head_tm=512, head_tn=256, n_temps=6, row_ssq=ssq, row_chunks=2)
        h, hb, ssq = _fused_linear("out_proj", i, [mix], [(w_out, 0, 0)], [h],
                                   lambda d, e, rs: e[0] + d[0], F32, D_MODEL,
                                   tm=1024, tn=512, head_tm=1024, head_tn=512, n_temps=2,
                                   next_gain=gain(ffn2_norm[i]), stats_bufs=(hb, ssq), row_chunks=4)

        h, hb, ssq = _swiglu_ffn(i, hb, ssq, ffn2_w_gate, ffn2_w_up, ffn2_w_down, h, gain(ple_norm[i]))

        last = i + 1 == DEPTH
        res = _fused_linear("ple", i, [hb, p_bf[i]], [(ple_w_gate, 0, 0), (ple_w_proj, 1, 0)], [h],
                            lambda d, e, rs: e[0] + sig(d[0] * rs) * d[1], F32, D_MODEL,
                            tm=1024, tn=512, head_tm=1024, head_tn=512, n_temps=4, row_ssq=ssq,
                            next_gain=None if last else gain(ffn1_norm[i + 1]),
                            stats_bufs=None if last else (mix, ssq_spare), row_chunks=4)
        h, hb, (ssq, ssq_spare) = (res, None, (None, None)) if last else (res[0], res[1], (res[2], ssq))
    out = _rmsnorm(h, final_norm, F32)
    return out.reshape(BATCH, SEQ, D_MODEL)
```

```python
import functools
import math

import jax
import jax.numpy as jnp
import numpy as np
from jax import lax
from jax.experimental import pallas as pl
from jax.experimental.pallas import tpu as pltpu

D_MODEL = 4096
BATCH = 4
SEQ = 2048
DEPTH = 2
CHUNK = 64
PLE_DIM = 256
D_FF = 11008
HEAD_DIM = 128
N_HEADS = D_MODEL // (2 * HEAD_DIM)
WIDTH = N_HEADS * HEAD_DIM
LEFT_CHUNKS = 8
REL_CLIP = 128
RMS_EPS = 1e-6
N_QKV = 6 * WIDTH
M_TOK = BATCH * SEQ
LOG2E = math.log2(math.e)
Q_SCALE = HEAD_DIM ** -0.5 * LOG2E

F32 = jnp.float32
BF16 = jnp.bfloat16

LANES = 128
V7X_VMEM_REQUEST_CAP = 60000 * 1024
MIB = 1024 * 1024

_NT = (((1,), (1,)), ((), ()))
_TN = (((0,), (0,)), ((), ()))


def _nbytes(shape, dtype):
    return int(np.prod([s for s in shape if s is not None])) * jnp.dtype(dtype).itemsize


def _vmem_limit(blocks, single=(), temps=()):
    total = 2 * sum(_nbytes(s, d) for s, d in blocks)
    total += sum(_nbytes(s, d) for s, d in single)
    total += sum(_nbytes(s, d) for s, d in temps)
    total += 4 * MIB
    return min(total, V7X_VMEM_REQUEST_CAP)


def _params(semantics, limit):
    return pltpu.CompilerParams(dimension_semantics=semantics, vmem_limit_bytes=limit)


def _rmsnorm_kernel(h_ref, g_ref, o_ref):
    x = h_ref[...]
    ms = jnp.mean(x * x, axis=-1, keepdims=True)
    o_ref[...] = (x * lax.rsqrt(ms + RMS_EPS) * g_ref[...]).astype(o_ref.dtype)


def _rmsnorm(h, g, out_dtype, tr=512):
    m, d = h.shape
    blocks = [((tr, d), F32), ((1, d), F32), ((tr, d), out_dtype)]
    return pl.pallas_call(
        _rmsnorm_kernel,
        grid=(m // tr,),
        in_specs=[pl.BlockSpec((tr, d), lambda i: (i, 0)),
                  pl.BlockSpec((1, d), lambda i: (0, 0))],
        out_specs=pl.BlockSpec((tr, d), lambda i: (i, 0)),
        out_shape=jax.ShapeDtypeStruct((m, d), out_dtype),
        compiler_params=_params(("parallel",), _vmem_limit(blocks, temps=[((tr, d), F32)] * 2)),
        name="rmsnorm",
    )(h, g.reshape(1, d))


def _row_scale(ssq):
    return lax.rsqrt(ssq * (1.0 / D_MODEL) + RMS_EPS)


def _prenorm_kernel(h_ref, g_ref, h_out_ref, hb_ref, ssq_ref, spare_ref):
    x = h_ref[...]
    h_out_ref[...] = x
    hb_ref[...] = (x * g_ref[...]).astype(BF16)
    ssq = jnp.sum(x * x, axis=-1, keepdims=True)
    ssq_ref[...] = ssq
    spare_ref[...] = ssq


def _prenorm(h, g, tr=512):
    m, d = h.shape
    blocks = [((tr, d), F32), ((1, d), F32), ((tr, d), F32), ((tr, d), BF16)] + [((tr, LANES), F32)] * 2
    return pl.pallas_call(
        _prenorm_kernel,
        grid=(m // tr,),
        in_specs=[pl.BlockSpec((tr, d), lambda i: (i, 0)),
                  pl.BlockSpec((1, d), lambda i: (0, 0))],
        out_specs=[pl.BlockSpec((tr, d), lambda i: (i, 0)),
                   pl.BlockSpec((tr, d), lambda i: (i, 0)),
                   pl.BlockSpec((tr, 1), lambda i: (i, 0)),
                   pl.BlockSpec((tr, 1), lambda i: (i, 0))],
        out_shape=[jax.ShapeDtypeStruct((m, d), F32), jax.ShapeDtypeStruct((m, d), BF16),
                   jax.ShapeDtypeStruct((m, 1), F32), jax.ShapeDtypeStruct((m, 1), F32)],
        compiler_params=_params(("parallel",), _vmem_limit(blocks, temps=[((tr, d), F32)] * 2)),
        name="prenorm",
    )(h, g.reshape(1, d))


def _linear_kernel(*refs, w_act, n_acts, n_extras, epilogue, cast, scaled, stats, n_alias, col_axis,
                   w_rows_are_outputs, row_chunks, tile_rows):
    nw = len(w_act)
    it = iter(refs)
    take = lambda n: [next(it) for _ in range(n)]
    acts, ws, extras = take(n_acts), take(nw), take(n_extras)
    ssq_in = take(1)[0] if scaled else None
    gain = take(1)[0] if stats else None
    take(n_alias)
    out = take(1)[0]
    hb, ssq_out = take(2) if stats else (None, None)
    if cast:
        wbfs = take(nw)
        for w, wb in zip(ws, wbfs):
            wb[...] = w[...].astype(BF16)
        ws = wbfs
    dims = _NT if w_rows_are_outputs else (((1,), (0,)), ((), ()))
    rc = tile_rows // row_chunks
    fill = (out.shape[0] - tile_rows) // row_chunks
    parts = []
    for c in range(row_chunks):
        rows = slice(c * rc, (c + 1) * rc)
        if fill:
            out[tile_rows + c * fill:tile_rows + (c + 1) * fill, :] = jnp.zeros(
                (fill, out.shape[1]), out.dtype)
        dots = [lax.dot_general(acts[ai][rows, :], w[...], dims, preferred_element_type=F32)
                for ai, w in zip(w_act, ws)]
        rs = _row_scale(ssq_in[rows, :]) if scaled else None
        res = epilogue(dots, [e[rows, :] for e in extras], rs)
        out[rows, :] = res.astype(out.dtype)
        if stats:
            hb[rows, :] = (res * gain[...]).astype(BF16)
            parts.append(jnp.sum(res * res, axis=1, keepdims=True))
    if stats:
        part = jnp.concatenate(parts, axis=0) if row_chunks > 1 else parts[0]
        col = pl.program_id(col_axis)

        @pl.when(col == 0)
        def _():
            ssq_out[...] = part

        @pl.when(col != 0)
        def _():
            ssq_out[...] += part


def _fused_linear(name, layer, acts, weights, extras, epilogue, out_dtype, n_out, *,
                  tm, tn, head_tm, head_tn, n_temps, row_ssq=None, next_gain=None, stats_bufs=None,
                  out_buf=None, w_rows_are_outputs=False, row_chunks=1):
    m = acts[0].shape[0]
    w_act = tuple(ai for _, ai, _ in weights)
    wt = w_rows_are_outputs
    ks = [w.shape[2 if wt else 1] for w, _, _ in weights]
    scaled, stats = row_ssq is not None, next_gain is not None
    in_place = len(extras) == 1
    assert len(extras) <= 1 and (in_place or not stats) and stats == (stats_bufs is not None)
    assert out_buf is None or not in_place
    kern = functools.partial(_linear_kernel, w_act=w_act, n_acts=len(acts), n_extras=len(extras),
                             epilogue=epilogue, scaled=scaled, stats=stats, w_rows_are_outputs=wt,
                             row_chunks=row_chunks)
    w_blk = lambda k, n: (n, k) if wt else (k, n)
    w_idx = lambda c: (c, 0) if wt else (0, c)
    side_in = ([row_ssq] if scaled else []) + ([next_gain] if stats else [])

    def specs(tm_, tn_, row, colblk):
        ins = [pl.BlockSpec((tm_, tn_), lambda *g: (row(*g), colblk(*g))) for _ in extras]
        ins += [pl.BlockSpec((tm_, 1), lambda *g: (row(*g), 0))] if scaled else []
        ins += [pl.BlockSpec((1, tn_), lambda *g: (0, colblk(*g)))] if stats else []
        outs = [pl.BlockSpec((tm_, tn_), lambda *g: (row(*g), colblk(*g)))]
        shapes = [jax.ShapeDtypeStruct((m, n_out), out_dtype)]
        blocks = [((tm_, tn_), F32)] * len(extras) + [((tm_, tn_), out_dtype)]
        if stats:
            outs += [pl.BlockSpec((tm_, tn_), lambda *g: (row(*g), colblk(*g))),
                     pl.BlockSpec((tm_, 1), lambda *g: (row(*g), 0))]
            shapes += [jax.ShapeDtypeStruct((m, n_out), BF16), jax.ShapeDtypeStruct((m, 1), F32)]
            blocks += [((tm_, tn_), BF16), ((tm_, LANES), F32)]
        blocks += [((tm_, LANES), F32)] if scaled else []
        return ins, outs, shapes, blocks

    for e in extras:
        assert e.shape[1] == n_out
    ins, outs, shapes, blocks = specs(head_tm, head_tn, lambda j: 0, lambda j: j)
    in_specs = [pl.BlockSpec((head_tm, a.shape[1]), lambda j: (0, 0), pipeline_mode=pl.Buffered(1))
                for a in acts]
    in_specs += [pl.BlockSpec((None,) + w_blk(k, head_tn),
                              lambda j, off=c0 // head_tn: (layer,) + w_idx(off + j))
                 for k, (_, _, c0) in zip(ks, weights)]
    n_in = len(in_specs) + len(ins)
    if in_place:
        aliases = {len(in_specs): 0}
        aliases.update({n_in + r: 1 + r for r in range(2 if stats else 0)})
    elif out_buf is not None:
        aliases = {n_in: 0}
    else:
        aliases = {}
        outs[0] = pl.BlockSpec((m, head_tn), lambda j: (0, j))
        blocks += [((m - head_tm, head_tn), out_dtype)]
    outs += [pl.BlockSpec(w_blk(k, head_tn), lambda j: w_idx(j)) for k in ks]
    shapes += [jax.ShapeDtypeStruct(w_blk(k, n_out), BF16) for k in ks]
    blocks += [((k, head_tn), F32) for k in ks] + [((k, head_tn), BF16) for k in ks]
    bufs = list(stats_bufs) if stats else [] if out_buf is None else [out_buf]
    res = pl.pallas_call(
        functools.partial(kern, cast=True, n_alias=len(bufs), col_axis=0, tile_rows=head_tm),
        grid=(n_out // head_tn,),
        in_specs=in_specs + ins + [pl.BlockSpec(memory_space=pl.ANY)] * len(bufs),
        out_specs=outs, out_shape=shapes, input_output_aliases=aliases,
        compiler_params=_params(("arbitrary",),
                                _vmem_limit(blocks, single=[((head_tm, a.shape[1]), BF16) for a in acts],
                                            temps=[((head_tm, head_tn), F32)] * n_temps)),
        name=name + "_head",
    )(*acts, *[w for w, _, _ in weights], *extras, *side_in, *bufs)
    n_res = 3 if stats else 1
    prior, wbfs = res[:n_res], res[n_res:]

    r0 = head_tm // tm
    ins, outs, shapes, blocks = specs(tm, tn, lambda i, j: i + r0, lambda i, j: j)
    in_specs = [pl.BlockSpec((tm, a.shape[1]), lambda i, j: (i + r0, 0)) for a in acts]
    in_specs += [pl.BlockSpec(w_blk(k, tn), lambda i, j: w_idx(j)) for k in ks]
    n_in = len(in_specs) + len(ins)
    blocks += [((tm, a.shape[1]), BF16) for a in acts] + [((k, tn), BF16) for k in ks]
    if in_place:
        residual, unread = [prior[0]], list(prior[1:])
        aliases = {len(in_specs): 0}
        aliases.update({n_in + r: 1 + r for r in range(len(unread))})
    else:
        residual, unread = [], list(prior)
        aliases = {n_in: 0}
    res = pl.pallas_call(
        functools.partial(kern, cast=False, n_alias=len(unread), col_axis=1, tile_rows=tm),
        grid=(m // tm - r0, n_out // tn),
        in_specs=in_specs + ins + [pl.BlockSpec(memory_space=pl.ANY)] * len(unread),
        out_specs=outs, out_shape=shapes, input_output_aliases=aliases,
        compiler_params=_params(("parallel", "arbitrary"),
                                _vmem_limit(blocks, temps=[((tm, tn), F32)] * n_temps)),
        name=name + "_tail",
    )(*acts, *wbfs, *residual, *side_in, *unread)
    return tuple(res) if stats else res[0]


def _swiglu_ffn(layer, hb, ssq, w_gate, w_up, w_down, h, next_gain, act_buf):
    def gate_up(d, e, rs):
        g = d[0] * rs
        return g * jax.nn.sigmoid(g) * (d[1] * rs)

    act = _fused_linear("ffn_gateup", layer, [hb], [(w_gate, 0, 0), (w_up, 0, 0)], [], gate_up,
                        BF16, D_FF, tm=2048, tn=256, head_tm=2048, head_tn=256, n_temps=4,
                        row_ssq=ssq, out_buf=act_buf, row_chunks=8)
    res = _fused_linear("ffn_down", layer, [act], [(w_down, 0, 0)], [h],
                        lambda d, e, rs: e[0] + 0.5 * d[0], F32, D_MODEL,
                        tm=512, tn=512, head_tm=512, head_tn=256, n_temps=2, next_gain=next_gain,
                        stats_bufs=(hb, ssq), row_chunks=4)
    return res + (act,)


def _log_sigmoid(x):
    return jnp.minimum(x, 0.0) - jnp.log1p(jnp.exp(-jnp.abs(x)))


def _forget_kernel(u_ref, ssq_ref, w_ref, b_ref, c_ref, carry_ref, *, ts):
    @pl.when(pl.program_id(1) == 0)
    def _():
        carry_ref[...] = jnp.zeros_like(carry_ref)

    w = w_ref[...].astype(BF16)
    w = jnp.concatenate([w, jnp.zeros((LANES - N_HEADS, w.shape[1]), BF16)], axis=0)
    logit = lax.dot_general(u_ref[0], w, _NT, preferred_element_type=F32)
    logit = logit * _row_scale(ssq_ref[0]) + b_ref[...]
    c = _log_sigmoid(logit)
    row = lax.broadcasted_iota(jnp.int32, c.shape, 0)
    shift = 1
    while shift < ts:
        c = c + jnp.where(row >= shift, pltpu.roll(c, shift, axis=0), 0.0)
        shift *= 2
    c = c + carry_ref[...]
    c_ref[0] = c
    carry_ref[...] = c[ts - 1:ts, :]


def _forget_cumsum(layer, u3, ssq3, w_in_t, bias, ts=512):
    b, s, d = u3.shape
    blocks = [((1, ts, d), BF16), ((1, ts, LANES), F32), ((N_HEADS, d), F32), ((1, LANES), F32),
              ((1, ts, LANES), F32)]
    return pl.pallas_call(
        functools.partial(_forget_kernel, ts=ts),
        grid=(b, s // ts),
        in_specs=[pl.BlockSpec((1, ts, d), lambda bi, si: (bi, si, 0)),
                  pl.BlockSpec((1, ts, 1), lambda bi, si: (bi, si, 0)),
                  pl.BlockSpec((None, N_HEADS, d), lambda bi, si: (layer, N_QKV // N_HEADS, 0)),
                  pl.BlockSpec((1, LANES), lambda bi, si: (0, 0))],
        out_specs=pl.BlockSpec((1, ts, LANES), lambda bi, si: (bi, si, 0)),
        out_shape=jax.ShapeDtypeStruct((b, s, LANES), F32),
        scratch_shapes=[pltpu.VMEM((1, LANES), F32)],
        compiler_params=_params(("parallel", "arbitrary"),
                                _vmem_limit(blocks, temps=[((ts, LANES), F32)] * 8 + [((LANES, d), BF16)])),
        name="forget_cumsum",
    )(u3, ssq3, w_in_t, bias)


_HPS = 2
_HW = _HPS * HEAD_DIM


def _head_cols(hh):
    return slice(hh * HEAD_DIM, (hh + 1) * HEAD_DIM)


def _fox_kernel(q_ref, k_ref, v_ref, c_ref, ct_ref, o_ref, *, tq):
    hp = pl.program_id(1)
    i = pl.program_id(2)
    q_start = pl.multiple_of(i * tq, tq)
    lane = lax.broadcasted_iota(jnp.int32, (tq, LANES), 1)
    q = [(q_ref[0, :, _head_cols(hh)].astype(F32) * Q_SCALE).astype(BF16)
         for hh in range(_HPS)]
    c_t = [ct_ref[0, hh, :, pl.ds(q_start, tq)] * LOG2E for hh in range(_HPS)]

    def block(j, carry, hh, masked):
        m, l, acc = carry
        start = pl.multiple_of(j * tq, tq)
        k = k_ref[0, pl.ds(start, tq), _head_cols(hh)]
        v = v_ref[0, pl.ds(start, tq), _head_cols(hh)]
        c_s = jnp.sum(jnp.where(lane == hp * _HPS + hh, c_ref[0, pl.ds(start, tq), :], 0.0),
                      axis=1, keepdims=True)
        x = lax.dot_general(k, q[hh], _NT, preferred_element_type=F32) - c_s * LOG2E
        if masked:
            key = lax.broadcasted_iota(jnp.int32, (tq, tq), 0)
            qry = lax.broadcasted_iota(jnp.int32, (tq, tq), 1)
            x = jnp.where(key <= qry, x, -jnp.inf)
        m_new = jnp.maximum(m, jnp.max(x, axis=0, keepdims=True) + c_t[hh])
        p = jnp.exp2(x + (c_t[hh] - m_new))
        alpha = jnp.exp2(m - m_new)
        l = alpha * l + jnp.sum(p, axis=0, keepdims=True)
        acc = alpha * acc + lax.dot_general(v, p.astype(BF16), _TN, preferred_element_type=F32)
        return m_new, l, acc

    def blocks(j, carries, masked):
        return tuple(block(j, carries[hh], hh, masked) for hh in range(_HPS))

    init = (jnp.full((1, tq), -jnp.inf, F32), jnp.zeros((1, tq), F32),
            jnp.zeros((HEAD_DIM, tq), F32))
    carries = lax.fori_loop(0, i, lambda j, c: blocks(j, c, False), (init,) * _HPS)
    carries = blocks(i, carries, True)
    for hh, (_, l, acc) in enumerate(carries):
        o_ref[0, :, _head_cols(hh)] = (acc / l).T.astype(o_ref.dtype)


def _fox_attention(z3, c, ct, tq=512):
    b, s, _ = z3.shape
    npair = N_HEADS // _HPS
    blocks = [((1, tq, _HW), BF16), ((1, s, _HW), BF16), ((1, s, _HW), BF16),
              ((1, s, LANES), F32), ((1, _HPS, 1, s), F32), ((1, tq, _HW), BF16)]
    return pl.pallas_call(
        functools.partial(_fox_kernel, tq=tq),
        grid=(b, npair, s // tq),
        in_specs=[pl.BlockSpec((1, tq, _HW), lambda bi, hp, i: (bi, i, hp)),
                  pl.BlockSpec((1, s, _HW), lambda bi, hp, i: (bi, 0, npair + hp)),
                  pl.BlockSpec((1, s, _HW), lambda bi, hp, i: (bi, 0, 2 * npair + hp)),
                  pl.BlockSpec((1, s, LANES), lambda bi, hp, i: (bi, 0, 0)),
                  pl.BlockSpec((1, _HPS, 1, s), lambda bi, hp, i: (bi, hp, 0, 0))],
        out_specs=pl.BlockSpec((1, tq, _HW), lambda bi, hp, i: (bi, i, hp)),
        out_shape=jax.ShapeDtypeStruct((b, s, WIDTH), BF16),
        compiler_params=_params(("parallel", "parallel", "arbitrary"),
                                _vmem_limit(blocks, temps=[((tq, tq), F32)] * 6 * _HPS)),
        name="fox_attention",
    )(z3, z3, z3, c, ct)


_QB = 8 * CHUNK
_N_BIAS_VEC = 12


def _bias_pieces_index():
    e = 128 * (np.arange(_N_BIAS_VEC)[:, None] - 4) + np.arange(LANES)[None, :]
    dist = _QB - e
    return np.clip(dist, -REL_CLIP, REL_CLIP) + REL_CLIP


def _chunk_kernel(q_ref, kp_ref, kc_ref, vp_ref, vc_ref, g_ref, o_ref, bias_ref):
    bi = pl.program_id(1)
    i = pl.program_id(2)

    @pl.when((bi == 0) & (i == 0))
    def _():
        r = lax.broadcasted_iota(jnp.int32, (LANES, LANES), 0)
        col = lax.broadcasted_iota(jnp.int32, (LANES, LANES), 1)
        upper = col >= r
        for hh in range(_HPS):
            rolled = [pltpu.roll(jnp.broadcast_to(g_ref[hh, k:k + 1, :] * LOG2E, (LANES, LANES)),
                                 0, 1, stride=1, stride_axis=0) for k in range(_N_BIAS_VEC)]
            for rb in range(_QB // LANES):
                for cb in range(2 * _QB // LANES):
                    delta = cb - rb + 4
                    tile = jnp.where(upper, rolled[delta], rolled[delta - 1])
                    q_chunk = (rb * LANES + r) // CHUNK
                    k_chunk = (cb * LANES + col) // CHUNK
                    ok = (k_chunk >= q_chunk) & (k_chunk <= q_chunk + LEFT_CHUNKS)
                    bias_ref[hh, cb * LANES:(cb + 1) * LANES, rb * LANES:(rb + 1) * LANES] = (
                        jnp.where(ok, tile, -jnp.inf).T)

    for hh in range(_HPS):
        cols = _head_cols(hh)
        q = (q_ref[0, :, cols].astype(F32) * Q_SCALE).astype(BF16)
        x_prev = (lax.dot_general(kp_ref[0, :, cols], q, _NT, preferred_element_type=F32)
                  + bias_ref[hh, :_QB, :])
        x_prev = jnp.where(i > 0, x_prev, -jnp.inf)
        x_cur = (lax.dot_general(kc_ref[0, :, cols], q, _NT, preferred_element_type=F32)
                 + bias_ref[hh, _QB:, :])
        m = jnp.maximum(jnp.max(x_prev, axis=0, keepdims=True),
                        jnp.max(x_cur, axis=0, keepdims=True))
        p_prev = jnp.exp2(x_prev - m)
        p_cur = jnp.exp2(x_cur - m)
        l = jnp.sum(p_prev, axis=0, keepdims=True) + jnp.sum(p_cur, axis=0, keepdims=True)
        acc = lax.dot_general(vp_ref[0, :, cols], p_prev.astype(BF16), _TN,
                              preferred_element_type=F32)
        acc = acc + lax.dot_general(vc_ref[0, :, cols], p_cur.astype(BF16), _TN,
                                    preferred_element_type=F32)
        o_ref[0, :, cols] = (acc / l).T.astype(o_ref.dtype)


def _chunk_attention(z3, bias_pieces):
    b, s, _ = z3.shape
    nb = s // _QB
    npair = N_HEADS // _HPS
    qo, ko, vo = 3 * npair, 4 * npair, 5 * npair
    blk = (1, _QB, _HW)
    blocks = [(blk, BF16)] * 6 + [((_HPS, 16, LANES), F32)]
    prev = lambda i: jnp.maximum(i - 1, 0)
    return pl.pallas_call(
        _chunk_kernel,
        grid=(npair, b, nb),
        in_specs=[pl.BlockSpec(blk, lambda hp, bi, i: (bi, i, qo + hp)),
                  pl.BlockSpec(blk, lambda hp, bi, i: (bi, prev(i), ko + hp)),
                  pl.BlockSpec(blk, lambda hp, bi, i: (bi, i, ko + hp)),
                  pl.BlockSpec(blk, lambda hp, bi, i: (bi, prev(i), vo + hp)),
                  pl.BlockSpec(blk, lambda hp, bi, i: (bi, i, vo + hp)),
                  pl.BlockSpec((_HPS, 16, LANES), lambda hp, bi, i: (hp, 0, 0))],
        out_specs=pl.BlockSpec(blk, lambda hp, bi, i: (bi, i, hp)),
        out_shape=jax.ShapeDtypeStruct((b, s, WIDTH), BF16),
        scratch_shapes=[pltpu.VMEM((_HPS, 2 * _QB, _QB), F32)],
        compiler_params=_params(("arbitrary", "arbitrary", "arbitrary"),
                                _vmem_limit(blocks, single=[((_HPS, 2 * _QB, _QB), F32)],
                                            temps=[((_QB, _QB), F32)] * 8 * _HPS)),
        name="chunk_attention",
    )(z3, z3, z3, z3, z3, bias_pieces)


def kernel(x, p, ffn1_norm, ffn1_w_gate, ffn1_w_up, ffn1_w_down, mix_norm, w_in, fox_forget_bias,
           rel_bias, w_branch_gate, w_proj_a, w_proj_b, w_out, ffn2_norm, ffn2_w_gate, ffn2_w_up,
           ffn2_w_down, ple_norm, ple_w_gate, ple_w_proj, final_norm):
    forget_bias = jnp.pad(fox_forget_bias, ((0, 0), (0, LANES - N_HEADS)))[:, None, :]
    p_bf = p.astype(BF16).reshape(DEPTH, M_TOK, PLE_DIM)
    bias_pieces = jnp.pad(rel_bias[:, :, _bias_pieces_index()],
                          ((0, 0), (0, 0), (0, 16 - _N_BIAS_VEC), (0, 0)))
    sig = jax.nn.sigmoid
    gain = lambda g: g.reshape(1, D_MODEL)
    w_in_t = jnp.swapaxes(w_in, 1, 2)

    h, hb, ssq, ssq_spare = _prenorm(x.reshape(M_TOK, D_MODEL), ffn1_norm[0])
    act_buf = z_buf = mix_buf = None
    for i in range(DEPTH):
        h, hb, ssq, act_buf = _swiglu_ffn(i, hb, ssq, ffn1_w_gate, ffn1_w_up, ffn1_w_down, h,
                                          gain(mix_norm[i]), act_buf)

        z = _fused_linear("qkv_proj", i, [hb], [(w_in_t, 0, 0)], [], lambda d, e, rs: d[0] * rs,
                          BF16, N_QKV, tm=1024, tn=1024, head_tm=1024, head_tn=512, n_temps=2,
                          row_ssq=ssq, out_buf=z_buf, w_rows_are_outputs=True, row_chunks=4)
        z3, z_buf = z.reshape(BATCH, SEQ, N_QKV), z
        c = _forget_cumsum(i, hb.reshape(BATCH, SEQ, D_MODEL), ssq.reshape(BATCH, SEQ, 1), w_in_t,
                           forget_bias[i])
        ct = c[:, :, :N_HEADS].transpose(0, 2, 1)[:, :, None, :]
        attn_a = _fox_attention(z3, c, ct).reshape(M_TOK, WIDTH)
        attn_b = _chunk_attention(z3, bias_pieces[i]).reshape(M_TOK, WIDTH)
        mix = _fused_linear(
            "branch_mix", i, [hb, attn_a, attn_b],
            [(w_branch_gate, 0, 0), (w_branch_gate, 0, D_MODEL), (w_proj_a, 1, 0), (w_proj_b, 2, 0)],
            [], lambda d, e, rs: sig(d[0] * rs) * d[2] + sig(d[1] * rs) * d[3], BF16, D_MODEL,
            tm=512, tn=512, head_tm=512, head_tn=256, n_temps=6, row_ssq=ssq, out_buf=mix_buf,
            row_chunks=2)
        h, hb, ssq = _fused_linear("out_proj", i, [mix], [(w_out, 0, 0)], [h],
                                   lambda d, e, rs: e[0] + d[0], F32, D_MODEL,
                                   tm=1024, tn=512, head_tm=1024, head_tn=512, n_temps=2,
                                   next_gain=gain(ffn2_norm[i]), stats_bufs=(hb, ssq), row_chunks=4)

        h, hb, ssq, act_buf = _swiglu_ffn(i, hb, ssq, ffn2_w_gate, ffn2_w_up, ffn2_w_down, h,
                                          gain(ple_norm[i]), act_buf)

        last = i + 1 == DEPTH
        res = _fused_linear("ple", i, [hb, p_bf[i]], [(ple_w_gate, 0, 0), (ple_w_proj, 1, 0)], [h],
                            lambda d, e, rs: e[0] + sig(d[0] * rs) * d[1], F32, D_MODEL,
                            tm=1024, tn=512, head_tm=1024, head_tn=512, n_temps=4, row_ssq=ssq,
                            next_gain=None if last else gain(ffn1_norm[i + 1]),
                            stats_bufs=None if last else (mix, ssq_spare), row_chunks=4)
        if not last:
            mix_buf, ssq_spare = hb, ssq
        h, hb, ssq = (res, None, None) if last else res
    out = _rmsnorm(h, final_norm, F32)
    return out.reshape(BATCH, SEQ, D_MODEL)
```

```python
import functools
import math

import jax
import jax.numpy as jnp
import numpy as np
from jax import lax
from jax.experimental import pallas as pl
from jax.experimental.pallas import tpu as pltpu

D_MODEL = 4096
BATCH = 4
SEQ = 2048
DEPTH = 2
CHUNK = 64
PLE_DIM = 256
D_FF = 11008
HEAD_DIM = 128
N_HEADS = D_MODEL // (2 * HEAD_DIM)
WIDTH = N_HEADS * HEAD_DIM
LEFT_CHUNKS = 8
REL_CLIP = 128
RMS_EPS = 1e-6
N_QKV = 6 * WIDTH
M_TOK = BATCH * SEQ
LOG2E = math.log2(math.e)
Q_SCALE = HEAD_DIM ** -0.5 * LOG2E

F32 = jnp.float32
BF16 = jnp.bfloat16

LANES = 128
V7X_VMEM_REQUEST_CAP = 60000 * 1024
MIB = 1024 * 1024

_NT = (((1,), (1,)), ((), ()))
_TN = (((0,), (0,)), ((), ()))


def _nbytes(shape, dtype):
    return int(np.prod([s for s in shape if s is not None])) * jnp.dtype(dtype).itemsize


def _vmem_limit(blocks, single=(), temps=()):
    total = 2 * sum(_nbytes(s, d) for s, d in blocks)
    total += sum(_nbytes(s, d) for s, d in single)
    total += sum(_nbytes(s, d) for s, d in temps)
    total += 4 * MIB
    return min(total, V7X_VMEM_REQUEST_CAP)


def _params(semantics, limit):
    return pltpu.CompilerParams(dimension_semantics=semantics, vmem_limit_bytes=limit)


def _rmsnorm_kernel(h_ref, g_ref, o_ref):
    x = h_ref[...]
    ms = jnp.mean(x * x, axis=-1, keepdims=True)
    o_ref[...] = (x * lax.rsqrt(ms + RMS_EPS) * g_ref[...]).astype(o_ref.dtype)


def _rmsnorm(h, g, out_dtype, tr=512):
    m, d = h.shape
    blocks = [((tr, d), F32), ((1, d), F32), ((tr, d), out_dtype)]
    return pl.pallas_call(
        _rmsnorm_kernel,
        grid=(m // tr,),
        in_specs=[pl.BlockSpec((tr, d), lambda i: (i, 0)),
                  pl.BlockSpec((1, d), lambda i: (0, 0))],
        out_specs=pl.BlockSpec((tr, d), lambda i: (i, 0)),
        out_shape=jax.ShapeDtypeStruct((m, d), out_dtype),
        compiler_params=_params(("parallel",), _vmem_limit(blocks, temps=[((tr, d), F32)] * 2)),
        name="rmsnorm",
    )(h, g.reshape(1, d))


def _row_scale(ssq):
    return lax.rsqrt(ssq * (1.0 / D_MODEL) + RMS_EPS)


def _prenorm_kernel(h_ref, g_ref, h_out_ref, hb_ref, ssq_ref, spare_ref):
    x = h_ref[...]
    h_out_ref[...] = x
    hb_ref[...] = (x * g_ref[...]).astype(BF16)
    ssq = jnp.sum(x * x, axis=-1, keepdims=True)
    ssq_ref[...] = ssq
    spare_ref[...] = ssq


def _prenorm(h, g, tr=512):
    m, d = h.shape
    blocks = [((tr, d), F32), ((1, d), F32), ((tr, d), F32), ((tr, d), BF16)] + [((tr, LANES), F32)] * 2
    return pl.pallas_call(
        _prenorm_kernel,
        grid=(m // tr,),
        in_specs=[pl.BlockSpec((tr, d), lambda i: (i, 0)),
                  pl.BlockSpec((1, d), lambda i: (0, 0))],
        out_specs=[pl.BlockSpec((tr, d), lambda i: (i, 0)),
                   pl.BlockSpec((tr, d), lambda i: (i, 0)),
                   pl.BlockSpec((tr, 1), lambda i: (i, 0)),
                   pl.BlockSpec((tr, 1), lambda i: (i, 0))],
        out_shape=[jax.ShapeDtypeStruct((m, d), F32), jax.ShapeDtypeStruct((m, d), BF16),
                   jax.ShapeDtypeStruct((m, 1), F32), jax.ShapeDtypeStruct((m, 1), F32)],
        compiler_params=_params(("parallel",), _vmem_limit(blocks, temps=[((tr, d), F32)] * 2)),
        name="prenorm",
    )(h, g.reshape(1, d))


def _linear_kernel(*refs, w_act, n_acts, n_extras, epilogue, cast, scaled, stats, n_alias, col_axis,
                   w_rows_are_outputs, row_chunks, tile_rows):
    nw = len(w_act)
    it = iter(refs)
    take = lambda n: [next(it) for _ in range(n)]
    acts, ws, extras = take(n_acts), take(nw), take(n_extras)
    ssq_in = take(1)[0] if scaled else None
    gain = take(1)[0] if stats else None
    take(n_alias)
    out = take(1)[0]
    hb, ssq_out = take(2) if stats else (None, None)
    if cast:
        wbfs = take(nw)
        for w, wb in zip(ws, wbfs):
            wb[...] = w[...].astype(BF16)
        ws = wbfs
    dims = _NT if w_rows_are_outputs else (((1,), (0,)), ((), ()))
    rc = tile_rows // row_chunks
    fill = (out.shape[0] - tile_rows) // row_chunks
    parts = []
    for c in range(row_chunks):
        rows = slice(c * rc, (c + 1) * rc)
        if fill:
            out[tile_rows + c * fill:tile_rows + (c + 1) * fill, :] = jnp.zeros(
                (fill, out.shape[1]), out.dtype)
        dots = [lax.dot_general(acts[ai][rows, :], w[...], dims, preferred_element_type=F32)
                for ai, w in zip(w_act, ws)]
        rs = _row_scale(ssq_in[rows, :]) if scaled else None
        res = epilogue(dots, [e[rows, :] for e in extras], rs)
        out[rows, :] = res.astype(out.dtype)
        if stats:
            hb[rows, :] = (res * gain[...]).astype(BF16)
            parts.append(jnp.sum(res * res, axis=1, keepdims=True))
    if stats:
        part = jnp.concatenate(parts, axis=0) if row_chunks > 1 else parts[0]
        col = pl.program_id(col_axis)

        @pl.when(col == 0)
        def _():
            ssq_out[...] = part

        @pl.when(col != 0)
        def _():
            ssq_out[...] += part


def _fused_linear(name, layer, acts, weights, extras, epilogue, out_dtype, n_out, *,
                  tm, tn, head_tm, head_tn, n_temps, row_ssq=None, next_gain=None, stats_bufs=None,
                  out_buf=None, w_rows_are_outputs=False, row_chunks=1):
    m = acts[0].shape[0]
    w_act = tuple(ai for _, ai, _ in weights)
    wt = w_rows_are_outputs
    ks = [w.shape[2 if wt else 1] for w, _, _ in weights]
    scaled, stats = row_ssq is not None, next_gain is not None
    in_place = len(extras) == 1
    assert len(extras) <= 1 and (in_place or not stats) and stats == (stats_bufs is not None)
    assert out_buf is None or not in_place
    kern = functools.partial(_linear_kernel, w_act=w_act, n_acts=len(acts), n_extras=len(extras),
                             epilogue=epilogue, scaled=scaled, stats=stats, w_rows_are_outputs=wt,
                             row_chunks=row_chunks)
    w_blk = lambda k, n: (n, k) if wt else (k, n)
    w_idx = lambda c: (c, 0) if wt else (0, c)
    side_in = ([row_ssq] if scaled else []) + ([next_gain] if stats else [])

    def specs(tm_, tn_, row, colblk):
        ins = [pl.BlockSpec((tm_, tn_), lambda *g: (row(*g), colblk(*g))) for _ in extras]
        ins += [pl.BlockSpec((tm_, 1), lambda *g: (row(*g), 0))] if scaled else []
        ins += [pl.BlockSpec((1, tn_), lambda *g: (0, colblk(*g)))] if stats else []
        outs = [pl.BlockSpec((tm_, tn_), lambda *g: (row(*g), colblk(*g)))]
        shapes = [jax.ShapeDtypeStruct((m, n_out), out_dtype)]
        blocks = [((tm_, tn_), F32)] * len(extras) + [((tm_, tn_), out_dtype)]
        if stats:
            outs += [pl.BlockSpec((tm_, tn_), lambda *g: (row(*g), colblk(*g))),
                     pl.BlockSpec((tm_, 1), lambda *g: (row(*g), 0))]
            shapes += [jax.ShapeDtypeStruct((m, n_out), BF16), jax.ShapeDtypeStruct((m, 1), F32)]
            blocks += [((tm_, tn_), BF16), ((tm_, LANES), F32)]
        blocks += [((tm_, LANES), F32)] if scaled else []
        return ins, outs, shapes, blocks

    for e in extras:
        assert e.shape[1] == n_out
    ins, outs, shapes, blocks = specs(head_tm, head_tn, lambda j: 0, lambda j: j)
    in_specs = [pl.BlockSpec((head_tm, a.shape[1]), lambda j: (0, 0), pipeline_mode=pl.Buffered(1))
                for a in acts]
    in_specs += [pl.BlockSpec((None,) + w_blk(k, head_tn),
                              lambda j, off=c0 // head_tn: (layer,) + w_idx(off + j))
                 for k, (_, _, c0) in zip(ks, weights)]
    n_in = len(in_specs) + len(ins)
    if in_place:
        aliases = {len(in_specs): 0}
        aliases.update({n_in + r: 1 + r for r in range(2 if stats else 0)})
    elif out_buf is not None:
        aliases = {n_in: 0}
    else:
        aliases = {}
        outs[0] = pl.BlockSpec((m, head_tn), lambda j: (0, j))
        blocks += [((m - head_tm, head_tn), out_dtype)]
    outs += [pl.BlockSpec(w_blk(k, head_tn), lambda j: w_idx(j)) for k in ks]
    shapes += [jax.ShapeDtypeStruct(w_blk(k, n_out), BF16) for k in ks]
    blocks += [((k, head_tn), F32) for k in ks] + [((k, head_tn), BF16) for k in ks]
    bufs = list(stats_bufs) if stats else [] if out_buf is None else [out_buf]
    res = pl.pallas_call(
        functools.partial(kern, cast=True, n_alias=len(bufs), col_axis=0, tile_rows=head_tm),
        grid=(n_out // head_tn,),
        in_specs=in_specs + ins + [pl.BlockSpec(memory_space=pl.ANY)] * len(bufs),
        out_specs=outs, out_shape=shapes, input_output_aliases=aliases,
        compiler_params=_params(("arbitrary",),
                                _vmem_limit(blocks, single=[((head_tm, a.shape[1]), BF16) for a in acts],
                                            temps=[((head_tm, head_tn), F32)] * n_temps)),
        name=name + "_head",
    )(*acts, *[w for w, _, _ in weights], *extras, *side_in, *bufs)
    n_res = 3 if stats else 1
    prior, wbfs = res[:n_res], res[n_res:]

    r0 = head_tm // tm
    ins, outs, shapes, blocks = specs(tm, tn, lambda i, j: i + r0, lambda i, j: j)
    in_specs = [pl.BlockSpec((tm, a.shape[1]), lambda i, j: (i + r0, 0)) for a in acts]
    in_specs += [pl.BlockSpec(w_blk(k, tn), lambda i, j: w_idx(j)) for k in ks]
    n_in = len(in_specs) + len(ins)
    blocks += [((tm, a.shape[1]), BF16) for a in acts] + [((k, tn), BF16) for k in ks]
    if in_place:
        residual, unread = [prior[0]], list(prior[1:])
        aliases = {len(in_specs): 0}
        aliases.update({n_in + r: 1 + r for r in range(len(unread))})
    else:
        residual, unread = [], list(prior)
        aliases = {n_in: 0}
    res = pl.pallas_call(
        functools.partial(kern, cast=False, n_alias=len(unread), col_axis=1, tile_rows=tm),
        grid=(m // tm - r0, n_out // tn),
        in_specs=in_specs + ins + [pl.BlockSpec(memory_space=pl.ANY)] * len(unread),
        out_specs=outs, out_shape=shapes, input_output_aliases=aliases,
        compiler_params=_params(("parallel", "arbitrary"),
                                _vmem_limit(blocks, temps=[((tm, tn), F32)] * n_temps)),
        name=name + "_tail",
    )(*acts, *wbfs, *residual, *side_in, *unread)
    return tuple(res) if stats else res[0]


def _ffn_down(layer, act, w_down, h, next_gain, stats_bufs, *, tm=512, tn=512, head_tm=1024,
              head_tn=256, row_chunks=4):
    m, k = act.shape
    d = w_down.shape[2]
    kh = k // 2
    kern = functools.partial(_linear_kernel, w_rows_are_outputs=False, scaled=False, row_chunks=row_chunks)
    head_act = lambda part: pl.BlockSpec((head_tm, kh), lambda j: (0, part), pipeline_mode=pl.Buffered(1))
    head_w = lambda part: pl.BlockSpec((None, kh, head_tn), lambda j: (layer, part, j))
    tile = pl.BlockSpec((head_tm, head_tn), lambda j: (0, j))
    w_out = pl.BlockSpec((kh, head_tn), lambda j: (0, j))
    w_blocks = [((kh, head_tn), F32), ((kh, head_tn), BF16)]
    single = [((head_tm, kh), BF16)]
    temps = [((head_tm, head_tn), F32)] * 2

    partial, wbf_a = pl.pallas_call(
        functools.partial(kern, w_act=(0,), n_acts=1, n_extras=0, epilogue=lambda dd, e, rs: dd[0],
                          cast=True, stats=False, n_alias=0, col_axis=0, tile_rows=head_tm),
        grid=(d // head_tn,),
        in_specs=[head_act(0), head_w(0)],
        out_specs=[tile, w_out],
        out_shape=[jax.ShapeDtypeStruct((head_tm, d), F32), jax.ShapeDtypeStruct((kh, d), BF16)],
        compiler_params=_params(("arbitrary",),
                                _vmem_limit(w_blocks + [((head_tm, head_tn), F32)], single=single, temps=temps)),
        name="ffn_down_head_a",
    )(act, w_down)

    stat_specs = [tile, pl.BlockSpec((head_tm, 1), lambda j: (0, 0))]
    h_new, hb, ssq, wbf_b = pl.pallas_call(
        functools.partial(kern, w_act=(0,), n_acts=1, n_extras=2,
                          epilogue=lambda dd, e, rs: e[0] + 0.5 * (e[1] + dd[0]),
                          cast=True, stats=True, n_alias=2, col_axis=0, tile_rows=head_tm),
        grid=(d // head_tn,),
        in_specs=[head_act(1), head_w(1), tile, tile, pl.BlockSpec((1, head_tn), lambda j: (0, j)),
                  pl.BlockSpec(memory_space=pl.ANY), pl.BlockSpec(memory_space=pl.ANY)],
        out_specs=[tile] + stat_specs + [w_out],
        out_shape=[jax.ShapeDtypeStruct((m, d), F32), jax.ShapeDtypeStruct((m, d), BF16),
                   jax.ShapeDtypeStruct((m, 1), F32), jax.ShapeDtypeStruct((kh, d), BF16)],
        input_output_aliases={2: 0, 5: 1, 6: 2},
        compiler_params=_params(("arbitrary",), _vmem_limit(
            w_blocks + [((head_tm, head_tn), F32)] * 3 + [((head_tm, head_tn), BF16), ((head_tm, LANES), F32)],
            single=single, temps=temps)),
        name="ffn_down_head_b",
    )(act, w_down, h, partial, next_gain, *stats_bufs)

    r0 = head_tm // tm
    tile = pl.BlockSpec((tm, tn), lambda i, j: (i + r0, j))
    acts_t = [pl.BlockSpec((tm, kh), lambda i, j, part=part: (i + r0, part)) for part in range(2)]
    w_t = pl.BlockSpec((kh, tn), lambda i, j: (0, j))
    return pl.pallas_call(
        functools.partial(kern, w_act=(0, 1), n_acts=2, n_extras=1,
                          epilogue=lambda dd, e, rs: e[0] + 0.5 * (dd[0] + dd[1]),
                          cast=False, stats=True, n_alias=2, col_axis=1, tile_rows=tm),
        grid=(m // tm - r0, d // tn),
        in_specs=acts_t + [w_t, w_t, tile, pl.BlockSpec((1, tn), lambda i, j: (0, j)),
                           pl.BlockSpec(memory_space=pl.ANY), pl.BlockSpec(memory_space=pl.ANY)],
        out_specs=[tile, tile, pl.BlockSpec((tm, 1), lambda i, j: (i + r0, 0))],
        out_shape=[jax.ShapeDtypeStruct((m, d), F32), jax.ShapeDtypeStruct((m, d), BF16),
                   jax.ShapeDtypeStruct((m, 1), F32)],
        input_output_aliases={4: 0, 6: 1, 7: 2},
        compiler_params=_params(("parallel", "arbitrary"), _vmem_limit(
            [((tm, kh), BF16), ((kh, tn), BF16)] * 2 + [((tm, tn), F32)] * 2
            + [((tm, tn), BF16), ((tm, LANES), F32)], temps=[((tm, tn), F32)] * 2)),
        name="ffn_down_tail",
    )(act, act, wbf_a, wbf_b, h_new, next_gain, hb, ssq)


def _swiglu_ffn(layer, hb, ssq, w_gate, w_up, w_down, h, next_gain, act_buf):
    def gate_up(d, e, rs):
        g = d[0] * rs
        return g * jax.nn.sigmoid(g) * (d[1] * rs)

    act = _fused_linear("ffn_gateup", layer, [hb], [(w_gate, 0, 0), (w_up, 0, 0)], [], gate_up,
                        BF16, D_FF, tm=2048, tn=256, head_tm=2048, head_tn=256, n_temps=4,
                        row_ssq=ssq, out_buf=act_buf, row_chunks=8)
    return tuple(_ffn_down(layer, act, w_down, h, next_gain, (hb, ssq))) + (act,)


def _log_sigmoid(x):
    return jnp.minimum(x, 0.0) - jnp.log1p(jnp.exp(-jnp.abs(x)))


def _forget_kernel(u_ref, ssq_ref, w_ref, b_ref, c_ref, carry_ref, *, ts):
    @pl.when(pl.program_id(1) == 0)
    def _():
        carry_ref[...] = jnp.zeros_like(carry_ref)

    w = w_ref[...].astype(BF16)
    w = jnp.concatenate([w, jnp.zeros((LANES - N_HEADS, w.shape[1]), BF16)], axis=0)
    logit = lax.dot_general(u_ref[0], w, _NT, preferred_element_type=F32)
    logit = logit * _row_scale(ssq_ref[0]) + b_ref[...]
    c = _log_sigmoid(logit)
    row = lax.broadcasted_iota(jnp.int32, c.shape, 0)
    shift = 1
    while shift < ts:
        c = c + jnp.where(row >= shift, pltpu.roll(c, shift, axis=0), 0.0)
        shift *= 2
    c = c + carry_ref[...]
    c_ref[0] = c
    carry_ref[...] = c[ts - 1:ts, :]


def _forget_cumsum(layer, u3, ssq3, w_in_t, bias, ts=512):
    b, s, d = u3.shape
    blocks = [((1, ts, d), BF16), ((1, ts, LANES), F32), ((N_HEADS, d), F32), ((1, LANES), F32),
              ((1, ts, LANES), F32)]
    return pl.pallas_call(
        functools.partial(_forget_kernel, ts=ts),
        grid=(b, s // ts),
        in_specs=[pl.BlockSpec((1, ts, d), lambda bi, si: (bi, si, 0)),
                  pl.BlockSpec((1, ts, 1), lambda bi, si: (bi, si, 0)),
                  pl.BlockSpec((None, N_HEADS, d), lambda bi, si: (layer, N_QKV // N_HEADS, 0)),
                  pl.BlockSpec((1, LANES), lambda bi, si: (0, 0))],
        out_specs=pl.BlockSpec((1, ts, LANES), lambda bi, si: (bi, si, 0)),
        out_shape=jax.ShapeDtypeStruct((b, s, LANES), F32),
        scratch_shapes=[pltpu.VMEM((1, LANES), F32)],
        compiler_params=_params(("parallel", "arbitrary"),
                                _vmem_limit(blocks, temps=[((ts, LANES), F32)] * 8 + [((LANES, d), BF16)])),
        name="forget_cumsum",
    )(u3, ssq3, w_in_t, bias)


_HPS = 2
_HW = _HPS * HEAD_DIM


def _head_cols(hh):
    return slice(hh * HEAD_DIM, (hh + 1) * HEAD_DIM)


def _fox_kernel(q_ref, k_ref, v_ref, c_ref, ct_ref, o_ref, *, tq):
    hp = pl.program_id(1)
    i = pl.program_id(2)
    q_start = pl.multiple_of(i * tq, tq)
    lane = lax.broadcasted_iota(jnp.int32, (tq, LANES), 1)
    q = [(q_ref[0, :, _head_cols(hh)].astype(F32) * Q_SCALE).astype(BF16)
         for hh in range(_HPS)]
    c_t = [ct_ref[0, hh, :, pl.ds(q_start, tq)] * LOG2E for hh in range(_HPS)]

    def block(j, carry, hh, masked):
        m, l, acc = carry
        start = pl.multiple_of(j * tq, tq)
        k = k_ref[0, pl.ds(start, tq), _head_cols(hh)]
        v = v_ref[0, pl.ds(start, tq), _head_cols(hh)]
        c_s = jnp.sum(jnp.where(lane == hp * _HPS + hh, c_ref[0, pl.ds(start, tq), :], 0.0),
                      axis=1, keepdims=True)
        x = lax.dot_general(k, q[hh], _NT, preferred_element_type=F32) - c_s * LOG2E
        if masked:
            key = lax.broadcasted_iota(jnp.int32, (tq, tq), 0)
            qry = lax.broadcasted_iota(jnp.int32, (tq, tq), 1)
            x = jnp.where(key <= qry, x, -jnp.inf)
        m_new = jnp.maximum(m, jnp.max(x, axis=0, keepdims=True) + c_t[hh])
        p = jnp.exp2(x + (c_t[hh] - m_new))
        alpha = jnp.exp2(m - m_new)
        l = alpha * l + jnp.sum(p, axis=0, keepdims=True)
        acc = alpha * acc + lax.dot_general(v, p.astype(BF16), _TN, preferred_element_type=F32)
        return m_new, l, acc

    def blocks(j, carries, masked):
        return tuple(block(j, carries[hh], hh, masked) for hh in range(_HPS))

    init = (jnp.full((1, tq), -jnp.inf, F32), jnp.zeros((1, tq), F32),
            jnp.zeros((HEAD_DIM, tq), F32))
    carries = lax.fori_loop(0, i, lambda j, c: blocks(j, c, False), (init,) * _HPS)
    carries = blocks(i, carries, True)
    for hh, (_, l, acc) in enumerate(carries):
        o_ref[0, :, _head_cols(hh)] = (acc / l).T.astype(o_ref.dtype)


def _fox_attention(z3, c, ct, tq=1024):
    b, s, _ = z3.shape
    npair = N_HEADS // _HPS
    blocks = [((1, tq, _HW), BF16), ((1, s, _HW), BF16), ((1, s, _HW), BF16),
              ((1, s, LANES), F32), ((1, _HPS, 1, s), F32), ((1, tq, _HW), BF16)]
    return pl.pallas_call(
        functools.partial(_fox_kernel, tq=tq),
        grid=(b, npair, s // tq),
        in_specs=[pl.BlockSpec((1, tq, _HW), lambda bi, hp, i: (bi, i, hp)),
                  pl.BlockSpec((1, s, _HW), lambda bi, hp, i: (bi, 0, npair + hp)),
                  pl.BlockSpec((1, s, _HW), lambda bi, hp, i: (bi, 0, 2 * npair + hp)),
                  pl.BlockSpec((1, s, LANES), lambda bi, hp, i: (bi, 0, 0)),
                  pl.BlockSpec((1, _HPS, 1, s), lambda bi, hp, i: (bi, hp, 0, 0))],
        out_specs=pl.BlockSpec((1, tq, _HW), lambda bi, hp, i: (bi, i, hp)),
        out_shape=jax.ShapeDtypeStruct((b, s, WIDTH), BF16),
        compiler_params=_params(("parallel", "parallel", "arbitrary"),
                                _vmem_limit(blocks, temps=[((tq, tq), F32)] * 6 * _HPS)),
        name="fox_attention",
    )(z3, z3, z3, c, ct)


_QB = 8 * CHUNK
_N_BIAS_VEC = 12


def _bias_pieces_index():
    e = 128 * (np.arange(_N_BIAS_VEC)[:, None] - 4) + np.arange(LANES)[None, :]
    dist = _QB - e
    return np.clip(dist, -REL_CLIP, REL_CLIP) + REL_CLIP


def _chunk_kernel(q_ref, kp_ref, kc_ref, vp_ref, vc_ref, g_ref, o_ref, bias_ref):
    bi = pl.program_id(1)
    i = pl.program_id(2)

    @pl.when((bi == 0) & (i == 0))
    def _():
        r = lax.broadcasted_iota(jnp.int32, (LANES, LANES), 0)
        col = lax.broadcasted_iota(jnp.int32, (LANES, LANES), 1)
        upper = col >= r
        for hh in range(_HPS):
            rolled = [pltpu.roll(jnp.broadcast_to(g_ref[hh, k:k + 1, :] * LOG2E, (LANES, LANES)),
                                 0, 1, stride=1, stride_axis=0) for k in range(_N_BIAS_VEC)]
            for rb in range(_QB // LANES):
                for cb in range(2 * _QB // LANES):
                    delta = cb - rb + 4
                    tile = jnp.where(upper, rolled[delta], rolled[delta - 1])
                    q_chunk = (rb * LANES + r) // CHUNK
                    k_chunk = (cb * LANES + col) // CHUNK
                    ok = (k_chunk >= q_chunk) & (k_chunk <= q_chunk + LEFT_CHUNKS)
                    bias_ref[hh, cb * LANES:(cb + 1) * LANES, rb * LANES:(rb + 1) * LANES] = (
                        jnp.where(ok, tile, -jnp.inf).T)

    for hh in range(_HPS):
        cols = _head_cols(hh)
        q = (q_ref[0, :, cols].astype(F32) * Q_SCALE).astype(BF16)
        x_prev = (lax.dot_general(kp_ref[0, :, cols], q, _NT, preferred_element_type=F32)
                  + bias_ref[hh, :_QB, :])
        x_prev = jnp.where(i > 0, x_prev, -jnp.inf)
        x_cur = (lax.dot_general(kc_ref[0, :, cols], q, _NT, preferred_element_type=F32)
                 + bias_ref[hh, _QB:, :])
        m = jnp.maximum(jnp.max(x_prev, axis=0, keepdims=True),
                        jnp.max(x_cur, axis=0, keepdims=True))
        p_prev = jnp.exp2(x_prev - m)
        p_cur = jnp.exp2(x_cur - m)
        l = jnp.sum(p_prev, axis=0, keepdims=True) + jnp.sum(p_cur, axis=0, keepdims=True)
        acc = lax.dot_general(vp_ref[0, :, cols], p_prev.astype(BF16), _TN,
                              preferred_element_type=F32)
        acc = acc + lax.dot_general(vc_ref[0, :, cols], p_cur.astype(BF16), _TN,
                                    preferred_element_type=F32)
        o_ref[0, :, cols] = (acc / l).T.astype(o_ref.dtype)


def _chunk_attention(z3, bias_pieces):
    b, s, _ = z3.shape
    nb = s // _QB
    npair = N_HEADS // _HPS
    qo, ko, vo = 3 * npair, 4 * npair, 5 * npair
    blk = (1, _QB, _HW)
    blocks = [(blk, BF16)] * 6 + [((_HPS, 16, LANES), F32)]
    prev = lambda i: jnp.maximum(i - 1, 0)
    return pl.pallas_call(
        _chunk_kernel,
        grid=(npair, b, nb),
        in_specs=[pl.BlockSpec(blk, lambda hp, bi, i: (bi, i, qo + hp)),
                  pl.BlockSpec(blk, lambda hp, bi, i: (bi, prev(i), ko + hp)),
                  pl.BlockSpec(blk, lambda hp, bi, i: (bi, i, ko + hp)),
                  pl.BlockSpec(blk, lambda hp, bi, i: (bi, prev(i), vo + hp)),
                  pl.BlockSpec(blk, lambda hp, bi, i: (bi, i, vo + hp)),
                  pl.BlockSpec((_HPS, 16, LANES), lambda hp, bi, i: (hp, 0, 0))],
        out_specs=pl.BlockSpec(blk, lambda hp, bi, i: (bi, i, hp)),
        out_shape=jax.ShapeDtypeStruct((b, s, WIDTH), BF16),
        scratch_shapes=[pltpu.VMEM((_HPS, 2 * _QB, _QB), F32)],
        compiler_params=_params(("arbitrary", "arbitrary", "arbitrary"),
                                _vmem_limit(blocks, single=[((_HPS, 2 * _QB, _QB), F32)],
                                            temps=[((_QB, _QB), F32)] * 8 * _HPS)),
        name="chunk_attention",
    )(z3, z3, z3, z3, z3, bias_pieces)


def kernel(x, p, ffn1_norm, ffn1_w_gate, ffn1_w_up, ffn1_w_down, mix_norm, w_in, fox_forget_bias,
           rel_bias, w_branch_gate, w_proj_a, w_proj_b, w_out, ffn2_norm, ffn2_w_gate, ffn2_w_up,
           ffn2_w_down, ple_norm, ple_w_gate, ple_w_proj, final_norm):
    forget_bias = jnp.pad(fox_forget_bias, ((0, 0), (0, LANES - N_HEADS)))[:, None, :]
    p_bf = p.astype(BF16).reshape(DEPTH, M_TOK, PLE_DIM)
    bias_pieces = jnp.pad(rel_bias[:, :, _bias_pieces_index()],
                          ((0, 0), (0, 0), (0, 16 - _N_BIAS_VEC), (0, 0)))
    sig = jax.nn.sigmoid
    gain = lambda g: g.reshape(1, D_MODEL)
    w_in_t = jnp.swapaxes(w_in, 1, 2)

    h, hb, ssq, ssq_spare = _prenorm(x.reshape(M_TOK, D_MODEL), ffn1_norm[0])
    act_buf = z_buf = mix_buf = None
    for i in range(DEPTH):
        h, hb, ssq, act_buf = _swiglu_ffn(i, hb, ssq, ffn1_w_gate, ffn1_w_up, ffn1_w_down, h,
                                          gain(mix_norm[i]), act_buf)

        z = _fused_linear("qkv_proj", i, [hb], [(w_in_t, 0, 0)], [], lambda d, e, rs: d[0] * rs,
                          BF16, N_QKV, tm=1024, tn=1024, head_tm=1024, head_tn=512, n_temps=2,
                          row_ssq=ssq, out_buf=z_buf, w_rows_are_outputs=True, row_chunks=4)
        z3, z_buf = z.reshape(BATCH, SEQ, N_QKV), z
        c = _forget_cumsum(i, hb.reshape(BATCH, SEQ, D_MODEL), ssq.reshape(BATCH, SEQ, 1), w_in_t,
                           forget_bias[i])
        ct = c[:, :, :N_HEADS].transpose(0, 2, 1)[:, :, None, :]
        attn_a = _fox_attention(z3, c, ct).reshape(M_TOK, WIDTH)
        attn_b = _chunk_attention(z3, bias_pieces[i]).reshape(M_TOK, WIDTH)
        mix = _fused_linear(
            "branch_mix", i, [hb, attn_a, attn_b],
            [(w_branch_gate, 0, 0), (w_branch_gate, 0, D_MODEL), (w_proj_a, 1, 0), (w_proj_b, 2, 0)],
            [], lambda d, e, rs: sig(d[0] * rs) * d[2] + sig(d[1] * rs) * d[3], BF16, D_MODEL,
            tm=512, tn=512, head_tm=512, head_tn=256, n_temps=6, row_ssq=ssq, out_buf=mix_buf,
            row_chunks=2)
        h, hb, ssq = _fused_linear("out_proj", i, [mix], [(w_out, 0, 0)], [h],
                                   lambda d, e, rs: e[0] + d[0], F32, D_MODEL,
                                   tm=1024, tn=512, head_tm=1024, head_tn=512, n_temps=2,
                                   next_gain=gain(ffn2_norm[i]), stats_bufs=(hb, ssq), row_chunks=4)

        h, hb, ssq, act_buf = _swiglu_ffn(i, hb, ssq, ffn2_w_gate, ffn2_w_up, ffn2_w_down, h,
                                          gain(ple_norm[i]), act_buf)

        last = i + 1 == DEPTH
        res = _fused_linear("ple", i, [hb, p_bf[i]], [(ple_w_gate, 0, 0), (ple_w_proj, 1, 0)], [h],
                            lambda d, e, rs: e[0] + sig(d[0] * rs) * d[1], F32, D_MODEL,
                            tm=1024, tn=512, head_tm=1024, head_tn=512, n_temps=4, row_ssq=ssq,
                            next_gain=None if last else gain(ffn1_norm[i + 1]),
                            stats_bufs=None if last else (mix, ssq_spare), row_chunks=4)
        if not last:
            mix_buf, ssq_spare = hb, ssq
        h, hb, ssq = (res, None, None) if last else res
    out = _rmsnorm(h, final_norm, F32)
    return out.reshape(BATCH, SEQ, D_MODEL)
```

```python
import functools
import math

import jax
import jax.numpy as jnp
import numpy as np
from jax import lax
from jax.experimental import pallas as pl
from jax.experimental.pallas import tpu as pltpu

D_MODEL = 4096
BATCH = 4
SEQ = 2048
DEPTH = 2
CHUNK = 64
PLE_DIM = 256
D_FF = 11008
HEAD_DIM = 128
N_HEADS = D_MODEL // (2 * HEAD_DIM)
WIDTH = N_HEADS * HEAD_DIM
LEFT_CHUNKS = 8
REL_CLIP = 128
RMS_EPS = 1e-6
N_QKV = 6 * WIDTH
M_TOK = BATCH * SEQ
LOG2E = math.log2(math.e)
Q_SCALE = HEAD_DIM ** -0.5 * LOG2E

F32 = jnp.float32
BF16 = jnp.bfloat16

LANES = 128
V7X_VMEM_REQUEST_CAP = 60000 * 1024
MIB = 1024 * 1024

_NT = (((1,), (1,)), ((), ()))
_TN = (((0,), (0,)), ((), ()))


def _nbytes(shape, dtype):
    return int(np.prod([s for s in shape if s is not None])) * jnp.dtype(dtype).itemsize


def _vmem_limit(blocks, single=(), temps=()):
    total = 2 * sum(_nbytes(s, d) for s, d in blocks)
    total += sum(_nbytes(s, d) for s, d in single)
    total += sum(_nbytes(s, d) for s, d in temps)
    total += 4 * MIB
    return min(total, V7X_VMEM_REQUEST_CAP)


def _params(semantics, limit):
    return pltpu.CompilerParams(dimension_semantics=semantics, vmem_limit_bytes=limit)


def _rmsnorm_kernel(h_ref, g_ref, o_ref):
    x = h_ref[...]
    ms = jnp.mean(x * x, axis=-1, keepdims=True)
    o_ref[...] = (x * lax.rsqrt(ms + RMS_EPS) * g_ref[...]).astype(o_ref.dtype)


def _rmsnorm(h, g, out_dtype, tr=512):
    m, d = h.shape
    blocks = [((tr, d), F32), ((1, d), F32), ((tr, d), out_dtype)]
    return pl.pallas_call(
        _rmsnorm_kernel,
        grid=(m // tr,),
        in_specs=[pl.BlockSpec((tr, d), lambda i: (i, 0)),
                  pl.BlockSpec((1, d), lambda i: (0, 0))],
        out_specs=pl.BlockSpec((tr, d), lambda i: (i, 0)),
        out_shape=jax.ShapeDtypeStruct((m, d), out_dtype),
        compiler_params=_params(("parallel",), _vmem_limit(blocks, temps=[((tr, d), F32)] * 2)),
        name="rmsnorm",
    )(h, g.reshape(1, d))


def _row_scale(ssq):
    return lax.rsqrt(ssq * (1.0 / D_MODEL) + RMS_EPS)


def _prenorm_kernel(h_ref, g_ref, h_out_ref, hb_ref, ssq_ref, spare_ref):
    x = h_ref[...]
    h_out_ref[...] = x
    hb_ref[...] = (x * g_ref[...]).astype(BF16)
    ssq = jnp.sum(x * x, axis=-1, keepdims=True)
    ssq_ref[...] = ssq
    spare_ref[...] = ssq


def _prenorm(h, g, tr=512):
    m, d = h.shape
    blocks = [((tr, d), F32), ((1, d), F32), ((tr, d), F32), ((tr, d), BF16)] + [((tr, LANES), F32)] * 2
    return pl.pallas_call(
        _prenorm_kernel,
        grid=(m // tr,),
        in_specs=[pl.BlockSpec((tr, d), lambda i: (i, 0)),
                  pl.BlockSpec((1, d), lambda i: (0, 0))],
        out_specs=[pl.BlockSpec((tr, d), lambda i: (i, 0)),
                   pl.BlockSpec((tr, d), lambda i: (i, 0)),
                   pl.BlockSpec((tr, 1), lambda i: (i, 0)),
                   pl.BlockSpec((tr, 1), lambda i: (i, 0))],
        out_shape=[jax.ShapeDtypeStruct((m, d), F32), jax.ShapeDtypeStruct((m, d), BF16),
                   jax.ShapeDtypeStruct((m, 1), F32), jax.ShapeDtypeStruct((m, 1), F32)],
        compiler_params=_params(("parallel",), _vmem_limit(blocks, temps=[((tr, d), F32)] * 2)),
        name="prenorm",
    )(h, g.reshape(1, d))


def _linear_kernel(*refs, w_act, n_acts, n_extras, epilogue, cast, scaled, stats, n_alias, col_axis,
                   w_rows_are_outputs, row_chunks, tile_rows):
    nw = len(w_act)
    it = iter(refs)
    take = lambda n: [next(it) for _ in range(n)]
    acts, ws, extras = take(n_acts), take(nw), take(n_extras)
    ssq_in = take(1)[0] if scaled else None
    gain = take(1)[0] if stats else None
    take(n_alias)
    out = take(1)[0]
    hb, ssq_out = take(2) if stats else (None, None)
    if cast:
        wbfs = take(nw)
        for w, wb in zip(ws, wbfs):
            wb[...] = w[...].astype(BF16)
        ws = wbfs
    dims = _NT if w_rows_are_outputs else (((1,), (0,)), ((), ()))
    rc = tile_rows // row_chunks
    fill = (out.shape[0] - tile_rows) // row_chunks
    parts = []
    for c in range(row_chunks):
        rows = slice(c * rc, (c + 1) * rc)
        if fill:
            out[tile_rows + c * fill:tile_rows + (c + 1) * fill, :] = jnp.zeros(
                (fill, out.shape[1]), out.dtype)
        dots = [lax.dot_general(acts[ai][rows, :], w[...], dims, preferred_element_type=F32)
                for ai, w in zip(w_act, ws)]
        rs = _row_scale(ssq_in[rows, :]) if scaled else None
        res = epilogue(dots, [e[rows, :] for e in extras], rs)
        out[rows, :] = res.astype(out.dtype)
        if stats:
            hb[rows, :] = (res * gain[...]).astype(BF16)
            parts.append(jnp.sum(res * res, axis=1, keepdims=True))
    if stats:
        part = jnp.concatenate(parts, axis=0) if row_chunks > 1 else parts[0]
        col = pl.program_id(col_axis)

        @pl.when(col == 0)
        def _():
            ssq_out[...] = part

        @pl.when(col != 0)
        def _():
            ssq_out[...] += part


def _fused_linear(name, layer, acts, weights, extras, epilogue, out_dtype, n_out, *,
                  tm, tn, head_tm, head_tn, n_temps, row_ssq=None, next_gain=None, stats_bufs=None,
                  out_buf=None, w_rows_are_outputs=False, row_chunks=1):
    m = acts[0].shape[0]
    w_act = tuple(ai for _, ai, _ in weights)
    wt = w_rows_are_outputs
    ks = [w.shape[2 if wt else 1] for w, _, _ in weights]
    scaled, stats = row_ssq is not None, next_gain is not None
    in_place = len(extras) == 1
    assert len(extras) <= 1 and (in_place or not stats) and stats == (stats_bufs is not None)
    assert out_buf is None or not in_place
    kern = functools.partial(_linear_kernel, w_act=w_act, n_acts=len(acts), n_extras=len(extras),
                             epilogue=epilogue, scaled=scaled, stats=stats, w_rows_are_outputs=wt,
                             row_chunks=row_chunks)
    w_blk = lambda k, n: (n, k) if wt else (k, n)
    w_idx = lambda c: (c, 0) if wt else (0, c)
    side_in = ([row_ssq] if scaled else []) + ([next_gain] if stats else [])

    def specs(tm_, tn_, row, colblk):
        ins = [pl.BlockSpec((tm_, tn_), lambda *g: (row(*g), colblk(*g))) for _ in extras]
        ins += [pl.BlockSpec((tm_, 1), lambda *g: (row(*g), 0))] if scaled else []
        ins += [pl.BlockSpec((1, tn_), lambda *g: (0, colblk(*g)))] if stats else []
        outs = [pl.BlockSpec((tm_, tn_), lambda *g: (row(*g), colblk(*g)))]
        shapes = [jax.ShapeDtypeStruct((m, n_out), out_dtype)]
        blocks = [((tm_, tn_), F32)] * len(extras) + [((tm_, tn_), out_dtype)]
        if stats:
            outs += [pl.BlockSpec((tm_, tn_), lambda *g: (row(*g), colblk(*g))),
                     pl.BlockSpec((tm_, 1), lambda *g: (row(*g), 0))]
            shapes += [jax.ShapeDtypeStruct((m, n_out), BF16), jax.ShapeDtypeStruct((m, 1), F32)]
            blocks += [((tm_, tn_), BF16), ((tm_, LANES), F32)]
        blocks += [((tm_, LANES), F32)] if scaled else []
        return ins, outs, shapes, blocks

    for e in extras:
        assert e.shape[1] == n_out
    ins, outs, shapes, blocks = specs(head_tm, head_tn, lambda j: 0, lambda j: j)
    in_specs = [pl.BlockSpec((head_tm, a.shape[1]), lambda j: (0, 0), pipeline_mode=pl.Buffered(1))
                for a in acts]
    in_specs += [pl.BlockSpec((None,) + w_blk(k, head_tn),
                              lambda j, off=c0 // head_tn: (layer,) + w_idx(off + j))
                 for k, (_, _, c0) in zip(ks, weights)]
    n_in = len(in_specs) + len(ins)
    if in_place:
        aliases = {len(in_specs): 0}
        aliases.update({n_in + r: 1 + r for r in range(2 if stats else 0)})
    elif out_buf is not None:
        aliases = {n_in: 0}
    else:
        aliases = {}
        outs[0] = pl.BlockSpec((m, head_tn), lambda j: (0, j))
        blocks += [((m - head_tm, head_tn), out_dtype)]
    outs += [pl.BlockSpec(w_blk(k, head_tn), lambda j: w_idx(j)) for k in ks]
    shapes += [jax.ShapeDtypeStruct(w_blk(k, n_out), BF16) for k in ks]
    blocks += [((k, head_tn), F32) for k in ks] + [((k, head_tn), BF16) for k in ks]
    bufs = list(stats_bufs) if stats else [] if out_buf is None else [out_buf]
    res = pl.pallas_call(
        functools.partial(kern, cast=True, n_alias=len(bufs), col_axis=0, tile_rows=head_tm),
        grid=(n_out // head_tn,),
        in_specs=in_specs + ins + [pl.BlockSpec(memory_space=pl.ANY)] * len(bufs),
        out_specs=outs, out_shape=shapes, input_output_aliases=aliases,
        compiler_params=_params(("arbitrary",),
                                _vmem_limit(blocks, single=[((head_tm, a.shape[1]), BF16) for a in acts],
                                            temps=[((head_tm, head_tn), F32)] * n_temps)),
        name=name + "_head",
    )(*acts, *[w for w, _, _ in weights], *extras, *side_in, *bufs)
    n_res = 3 if stats else 1
    prior, wbfs = res[:n_res], res[n_res:]

    r0 = head_tm // tm
    ins, outs, shapes, blocks = specs(tm, tn, lambda i, j: i + r0, lambda i, j: j)
    in_specs = [pl.BlockSpec((tm, a.shape[1]), lambda i, j: (i + r0, 0)) for a in acts]
    in_specs += [pl.BlockSpec(w_blk(k, tn), lambda i, j: w_idx(j)) for k in ks]
    n_in = len(in_specs) + len(ins)
    blocks += [((tm, a.shape[1]), BF16) for a in acts] + [((k, tn), BF16) for k in ks]
    if in_place:
        residual, unread = [prior[0]], list(prior[1:])
        aliases = {len(in_specs): 0}
        aliases.update({n_in + r: 1 + r for r in range(len(unread))})
    else:
        residual, unread = [], list(prior)
        aliases = {n_in: 0}
    res = pl.pallas_call(
        functools.partial(kern, cast=False, n_alias=len(unread), col_axis=1, tile_rows=tm),
        grid=(m // tm - r0, n_out // tn),
        in_specs=in_specs + ins + [pl.BlockSpec(memory_space=pl.ANY)] * len(unread),
        out_specs=outs, out_shape=shapes, input_output_aliases=aliases,
        compiler_params=_params(("parallel", "arbitrary"),
                                _vmem_limit(blocks, temps=[((tm, tn), F32)] * n_temps)),
        name=name + "_tail",
    )(*acts, *wbfs, *residual, *side_in, *unread)
    return tuple(res) if stats else res[0]


def _ffn_down(layer, act, w_down, h, next_gain, stats_bufs, *, tm=512, tn=512, head_tm=1024,
              head_tn=256, row_chunks=4):
    m, k = act.shape
    d = w_down.shape[2]
    kh = k // 2
    kern = functools.partial(_linear_kernel, w_rows_are_outputs=False, scaled=False, row_chunks=row_chunks)
    head_act = lambda part: pl.BlockSpec((head_tm, kh), lambda j: (0, part), pipeline_mode=pl.Buffered(1))
    head_w = lambda part: pl.BlockSpec((None, kh, head_tn), lambda j: (layer, part, j))
    tile = pl.BlockSpec((head_tm, head_tn), lambda j: (0, j))
    w_out = pl.BlockSpec((kh, head_tn), lambda j: (0, j))
    w_blocks = [((kh, head_tn), F32), ((kh, head_tn), BF16)]
    single = [((head_tm, kh), BF16)]
    temps = [((head_tm, head_tn), F32)] * 2

    partial, wbf_a = pl.pallas_call(
        functools.partial(kern, w_act=(0,), n_acts=1, n_extras=0, epilogue=lambda dd, e, rs: dd[0],
                          cast=True, stats=False, n_alias=0, col_axis=0, tile_rows=head_tm),
        grid=(d // head_tn,),
        in_specs=[head_act(0), head_w(0)],
        out_specs=[tile, w_out],
        out_shape=[jax.ShapeDtypeStruct((head_tm, d), F32), jax.ShapeDtypeStruct((kh, d), BF16)],
        compiler_params=_params(("arbitrary",),
                                _vmem_limit(w_blocks + [((head_tm, head_tn), F32)], single=single, temps=temps)),
        name="ffn_down_head_a",
    )(act, w_down)

    stat_specs = [tile, pl.BlockSpec((head_tm, 1), lambda j: (0, 0))]
    h_new, hb, ssq, wbf_b = pl.pallas_call(
        functools.partial(kern, w_act=(0,), n_acts=1, n_extras=2,
                          epilogue=lambda dd, e, rs: e[0] + 0.5 * (e[1] + dd[0]),
                          cast=True, stats=True, n_alias=2, col_axis=0, tile_rows=head_tm),
        grid=(d // head_tn,),
        in_specs=[head_act(1), head_w(1), tile, tile, pl.BlockSpec((1, head_tn), lambda j: (0, j)),
                  pl.BlockSpec(memory_space=pl.ANY), pl.BlockSpec(memory_space=pl.ANY)],
        out_specs=[tile] + stat_specs + [w_out],
        out_shape=[jax.ShapeDtypeStruct((m, d), F32), jax.ShapeDtypeStruct((m, d), BF16),
                   jax.ShapeDtypeStruct((m, 1), F32), jax.ShapeDtypeStruct((kh, d), BF16)],
        input_output_aliases={2: 0, 5: 1, 6: 2},
        compiler_params=_params(("arbitrary",), _vmem_limit(
            w_blocks + [((head_tm, head_tn), F32)] * 3 + [((head_tm, head_tn), BF16), ((head_tm, LANES), F32)],
            single=single, temps=temps)),
        name="ffn_down_head_b",
    )(act, w_down, h, partial, next_gain, *stats_bufs)

    r0 = head_tm // tm
    tile = pl.BlockSpec((tm, tn), lambda i, j: (i + r0, j))
    acts_t = [pl.BlockSpec((tm, kh), lambda i, j, part=part: (i + r0, part)) for part in range(2)]
    w_t = pl.BlockSpec((kh, tn), lambda i, j: (0, j))
    return pl.pallas_call(
        functools.partial(kern, w_act=(0, 1), n_acts=2, n_extras=1,
                          epilogue=lambda dd, e, rs: e[0] + 0.5 * (dd[0] + dd[1]),
                          cast=False, stats=True, n_alias=2, col_axis=1, tile_rows=tm),
        grid=(m // tm - r0, d // tn),
        in_specs=acts_t + [w_t, w_t, tile, pl.BlockSpec((1, tn), lambda i, j: (0, j)),
                           pl.BlockSpec(memory_space=pl.ANY), pl.BlockSpec(memory_space=pl.ANY)],
        out_specs=[tile, tile, pl.BlockSpec((tm, 1), lambda i, j: (i + r0, 0))],
        out_shape=[jax.ShapeDtypeStruct((m, d), F32), jax.ShapeDtypeStruct((m, d), BF16),
                   jax.ShapeDtypeStruct((m, 1), F32)],
        input_output_aliases={4: 0, 6: 1, 7: 2},
        compiler_params=_params(("parallel", "arbitrary"), _vmem_limit(
            [((tm, kh), BF16), ((kh, tn), BF16)] * 2 + [((tm, tn), F32)] * 2
            + [((tm, tn), BF16), ((tm, LANES), F32)], temps=[((tm, tn), F32)] * 2)),
        name="ffn_down_tail",
    )(act, act, wbf_a, wbf_b, h_new, next_gain, hb, ssq)


def _swiglu_ffn(layer, hb, ssq, w_gate, w_up, w_down, h, next_gain, act_buf):
    def gate_up(d, e, rs):
        g = d[0] * rs
        return g * jax.nn.sigmoid(g) * (d[1] * rs)

    act = _fused_linear("ffn_gateup", layer, [hb], [(w_gate, 0, 0), (w_up, 0, 0)], [], gate_up,
                        BF16, D_FF, tm=2048, tn=256, head_tm=2048, head_tn=256, n_temps=4,
                        row_ssq=ssq, out_buf=act_buf, row_chunks=8)
    return tuple(_ffn_down(layer, act, w_down, h, next_gain, (hb, ssq))) + (act,)


def _log_sigmoid(x):
    return jnp.minimum(x, 0.0) - jnp.log1p(jnp.exp(-jnp.abs(x)))


def _forget_kernel(u_ref, ssq_ref, w_ref, b_ref, c_ref, carry_ref, *, ts):
    @pl.when(pl.program_id(1) == 0)
    def _():
        carry_ref[...] = jnp.zeros_like(carry_ref)

    w = w_ref[...].astype(BF16)
    w = jnp.concatenate([w, jnp.zeros((LANES - N_HEADS, w.shape[1]), BF16)], axis=0)
    logit = lax.dot_general(u_ref[0], w, _NT, preferred_element_type=F32)
    logit = logit * _row_scale(ssq_ref[0]) + b_ref[...]
    c = _log_sigmoid(logit)
    row = lax.broadcasted_iota(jnp.int32, c.shape, 0)
    shift = 1
    while shift < ts:
        c = c + jnp.where(row >= shift, pltpu.roll(c, shift, axis=0), 0.0)
        shift *= 2
    c = c + carry_ref[...]
    c_ref[0] = c
    carry_ref[...] = c[ts - 1:ts, :]


def _forget_cumsum(layer, u3, ssq3, w_in_t, bias, ts=512):
    b, s, d = u3.shape
    blocks = [((1, ts, d), BF16), ((1, ts, LANES), F32), ((N_HEADS, d), F32), ((1, LANES), F32),
              ((1, ts, LANES), F32)]
    return pl.pallas_call(
        functools.partial(_forget_kernel, ts=ts),
        grid=(b, s // ts),
        in_specs=[pl.BlockSpec((1, ts, d), lambda bi, si: (bi, si, 0)),
                  pl.BlockSpec((1, ts, 1), lambda bi, si: (bi, si, 0)),
                  pl.BlockSpec((None, N_HEADS, d), lambda bi, si: (layer, N_QKV // N_HEADS, 0)),
                  pl.BlockSpec((1, LANES), lambda bi, si: (0, 0))],
        out_specs=pl.BlockSpec((1, ts, LANES), lambda bi, si: (bi, si, 0)),
        out_shape=jax.ShapeDtypeStruct((b, s, LANES), F32),
        scratch_shapes=[pltpu.VMEM((1, LANES), F32)],
        compiler_params=_params(("parallel", "arbitrary"),
                                _vmem_limit(blocks, temps=[((ts, LANES), F32)] * 8 + [((LANES, d), BF16)])),
        name="forget_cumsum",
    )(u3, ssq3, w_in_t, bias)


_HPS = 2
_HW = _HPS * HEAD_DIM


def _head_cols(hh):
    return slice(hh * HEAD_DIM, (hh + 1) * HEAD_DIM)


def _fox_kernel(q_ref, k_ref, v_ref, c_ref, ct_ref, o_ref, *, tq):
    hp = pl.program_id(1)
    i = pl.program_id(2)
    q_start = pl.multiple_of(i * tq, tq)
    half = tq // 2
    q = [(q_ref[0, :, _head_cols(hh)].astype(F32) * Q_SCALE).astype(BF16)
         for hh in range(_HPS)]
    c_t = [ct_ref[0, hh, :, pl.ds(q_start, tq)] * LOG2E for hh in range(_HPS)]

    def block(start, nk, q_lo, carry, hh, diag_offset=None):
        m, l, acc = carry
        nq = tq - q_lo
        k = k_ref[0, pl.ds(start, nk), _head_cols(hh)]
        v = v_ref[0, pl.ds(start, nk), _head_cols(hh)]
        lane = lax.broadcasted_iota(jnp.int32, (nk, LANES), 1)
        c_s = jnp.sum(jnp.where(lane == hp * _HPS + hh, c_ref[0, pl.ds(start, nk), :], 0.0),
                      axis=1, keepdims=True)
        x = lax.dot_general(k, q[hh][q_lo:, :], _NT, preferred_element_type=F32) - c_s * LOG2E
        if diag_offset is not None:
            key = lax.broadcasted_iota(jnp.int32, (nk, nq), 0) + diag_offset
            qry = lax.broadcasted_iota(jnp.int32, (nk, nq), 1) + q_lo
            x = jnp.where(key <= qry, x, -jnp.inf)
        ct = c_t[hh][:, q_lo:]
        m_new = jnp.maximum(m, jnp.max(x, axis=0, keepdims=True) + ct)
        p = jnp.exp2(x + (ct - m_new))
        alpha = jnp.exp2(m - m_new)
        l = alpha * l + jnp.sum(p, axis=0, keepdims=True)
        acc = alpha * acc + lax.dot_general(v, p.astype(BF16), _TN, preferred_element_type=F32)
        return m_new, l, acc

    def full_blocks(j, carries):
        start = pl.multiple_of(j * tq, tq)
        return tuple(block(start, tq, 0, carries[hh], hh) for hh in range(_HPS))

    init = (jnp.full((1, tq), -jnp.inf, F32), jnp.zeros((1, tq), F32),
            jnp.zeros((HEAD_DIM, tq), F32))
    carries = lax.fori_loop(0, i, full_blocks, (init,) * _HPS)
    for hh in range(_HPS):
        carry = block(q_start, half, 0, carries[hh], hh, diag_offset=0)
        late = block(pl.multiple_of(q_start + half, half), half, half,
                     tuple(t[:, half:] for t in carry), hh, diag_offset=half)
        _, l, acc = (jnp.concatenate([t[:, :half], u], axis=1) for t, u in zip(carry, late))
        o_ref[0, :, _head_cols(hh)] = (acc / l).T.astype(o_ref.dtype)


def _fox_attention(z3, c, ct, tq=1024):
    b, s, _ = z3.shape
    npair = N_HEADS // _HPS
    blocks = [((1, tq, _HW), BF16), ((1, s, _HW), BF16), ((1, s, _HW), BF16),
              ((1, s, LANES), F32), ((1, _HPS, 1, s), F32), ((1, tq, _HW), BF16)]
    return pl.pallas_call(
        functools.partial(_fox_kernel, tq=tq),
        grid=(b, npair, s // tq),
        in_specs=[pl.BlockSpec((1, tq, _HW), lambda bi, hp, i: (bi, i, hp)),
                  pl.BlockSpec((1, s, _HW), lambda bi, hp, i: (bi, 0, npair + hp)),
                  pl.BlockSpec((1, s, _HW), lambda bi, hp, i: (bi, 0, 2 * npair + hp)),
                  pl.BlockSpec((1, s, LANES), lambda bi, hp, i: (bi, 0, 0)),
                  pl.BlockSpec((1, _HPS, 1, s), lambda bi, hp, i: (bi, hp, 0, 0))],
        out_specs=pl.BlockSpec((1, tq, _HW), lambda bi, hp, i: (bi, i, hp)),
        out_shape=jax.ShapeDtypeStruct((b, s, WIDTH), BF16),
        compiler_params=_params(("parallel", "parallel", "arbitrary"),
                                _vmem_limit(blocks, temps=[((tq, tq), F32)] * 6 * _HPS)),
        name="fox_attention",
    )(z3, z3, z3, c, ct)


_QB = 8 * CHUNK
_N_BIAS_VEC = 12


def _bias_pieces_index():
    e = 128 * (np.arange(_N_BIAS_VEC)[:, None] - 4) + np.arange(LANES)[None, :]
    dist = _QB - e
    return np.clip(dist, -REL_CLIP, REL_CLIP) + REL_CLIP


def _chunk_kernel(q_ref, kp_ref, kc_ref, vp_ref, vc_ref, g_ref, o_ref, bias_ref):
    bi = pl.program_id(1)
    i = pl.program_id(2)

    @pl.when((bi == 0) & (i == 0))
    def _():
        r = lax.broadcasted_iota(jnp.int32, (LANES, LANES), 0)
        col = lax.broadcasted_iota(jnp.int32, (LANES, LANES), 1)
        upper = col >= r
        for hh in range(_HPS):
            rolled = [pltpu.roll(jnp.broadcast_to(g_ref[hh, k:k + 1, :] * LOG2E, (LANES, LANES)),
                                 0, 1, stride=1, stride_axis=0) for k in range(_N_BIAS_VEC)]
            for rb in range(_QB // LANES):
                for cb in range(2 * _QB // LANES):
                    delta = cb - rb + 4
                    tile = jnp.where(upper, rolled[delta], rolled[delta - 1])
                    q_chunk = (rb * LANES + r) // CHUNK
                    k_chunk = (cb * LANES + col) // CHUNK
                    ok = (k_chunk >= q_chunk) & (k_chunk <= q_chunk + LEFT_CHUNKS)
                    bias_ref[hh, cb * LANES:(cb + 1) * LANES, rb * LANES:(rb + 1) * LANES] = (
                        jnp.where(ok, tile, -jnp.inf).T)

    def attend(with_prev):
        for hh in range(_HPS):
            cols = _head_cols(hh)
            q = (q_ref[0, :, cols].astype(F32) * Q_SCALE).astype(BF16)
            x_cur = (lax.dot_general(kc_ref[0, :, cols], q, _NT, preferred_element_type=F32)
                     + bias_ref[hh, _QB:, :])
            m = jnp.max(x_cur, axis=0, keepdims=True)
            if with_prev:
                x_prev = (lax.dot_general(kp_ref[0, :, cols], q, _NT, preferred_element_type=F32)
                          + bias_ref[hh, :_QB, :])
                m = jnp.maximum(m, jnp.max(x_prev, axis=0, keepdims=True))
            p_cur = jnp.exp2(x_cur - m)
            l = jnp.sum(p_cur, axis=0, keepdims=True)
            acc = lax.dot_general(vc_ref[0, :, cols], p_cur.astype(BF16), _TN,
                                  preferred_element_type=F32)
            if with_prev:
                p_prev = jnp.exp2(x_prev - m)
                l = l + jnp.sum(p_prev, axis=0, keepdims=True)
                acc = acc + lax.dot_general(vp_ref[0, :, cols], p_prev.astype(BF16), _TN,
                                            preferred_element_type=F32)
            o_ref[0, :, cols] = (acc / l).T.astype(o_ref.dtype)

    @pl.when(i == 0)
    def _():
        attend(False)

    @pl.when(i > 0)
    def _():
        attend(True)


def _chunk_attention(z3, bias_pieces):
    b, s, _ = z3.shape
    nb = s // _QB
    npair = N_HEADS // _HPS
    qo, ko, vo = 3 * npair, 4 * npair, 5 * npair
    blk = (1, _QB, _HW)
    blocks = [(blk, BF16)] * 6 + [((_HPS, 16, LANES), F32)]
    prev = lambda i: jnp.maximum(i - 1, 0)
    return pl.pallas_call(
        _chunk_kernel,
        grid=(npair, b, nb),
        in_specs=[pl.BlockSpec(blk, lambda hp, bi, i: (bi, i, qo + hp)),
                  pl.BlockSpec(blk, lambda hp, bi, i: (bi, prev(i), ko + hp)),
                  pl.BlockSpec(blk, lambda hp, bi, i: (bi, i, ko + hp)),
                  pl.BlockSpec(blk, lambda hp, bi, i: (bi, prev(i), vo + hp)),
                  pl.BlockSpec(blk, lambda hp, bi, i: (bi, i, vo + hp)),
                  pl.BlockSpec((_HPS, 16, LANES), lambda hp, bi, i: (hp, 0, 0))],
        out_specs=pl.BlockSpec(blk, lambda hp, bi, i: (bi, i, hp)),
        out_shape=jax.ShapeDtypeStruct((b, s, WIDTH), BF16),
        scratch_shapes=[pltpu.VMEM((_HPS, 2 * _QB, _QB), F32)],
        compiler_params=_params(("arbitrary", "arbitrary", "arbitrary"),
                                _vmem_limit(blocks, single=[((_HPS, 2 * _QB, _QB), F32)],
                                            temps=[((_QB, _QB), F32)] * 8 * _HPS)),
        name="chunk_attention",
    )(z3, z3, z3, z3, z3, bias_pieces)


def kernel(x, p, ffn1_norm, ffn1_w_gate, ffn1_w_up, ffn1_w_down, mix_norm, w_in, fox_forget_bias,
           rel_bias, w_branch_gate, w_proj_a, w_proj_b, w_out, ffn2_norm, ffn2_w_gate, ffn2_w_up,
           ffn2_w_down, ple_norm, ple_w_gate, ple_w_proj, final_norm):
    forget_bias = jnp.pad(fox_forget_bias, ((0, 0), (0, LANES - N_HEADS)))[:, None, :]
    p_bf = p.astype(BF16).reshape(DEPTH, M_TOK, PLE_DIM)
    bias_pieces = jnp.pad(rel_bias[:, :, _bias_pieces_index()],
                          ((0, 0), (0, 0), (0, 16 - _N_BIAS_VEC), (0, 0)))
    sig = jax.nn.sigmoid
    gain = lambda g: g.reshape(1, D_MODEL)
    w_in_t = jnp.swapaxes(w_in, 1, 2)

    h, hb, ssq, ssq_spare = _prenorm(x.reshape(M_TOK, D_MODEL), ffn1_norm[0])
    act_buf = z_buf = mix_buf = None
    for i in range(DEPTH):
        h, hb, ssq, act_buf = _swiglu_ffn(i, hb, ssq, ffn1_w_gate, ffn1_w_up, ffn1_w_down, h,
                                          gain(mix_norm[i]), act_buf)

        z = _fused_linear("qkv_proj", i, [hb], [(w_in_t, 0, 0)], [], lambda d, e, rs: d[0] * rs,
                          BF16, N_QKV, tm=1024, tn=1024, head_tm=1024, head_tn=512, n_temps=2,
                          row_ssq=ssq, out_buf=z_buf, w_rows_are_outputs=True, row_chunks=4)
        z3, z_buf = z.reshape(BATCH, SEQ, N_QKV), z
        c = _forget_cumsum(i, hb.reshape(BATCH, SEQ, D_MODEL), ssq.reshape(BATCH, SEQ, 1), w_in_t,
                           forget_bias[i])
        ct = c[:, :, :N_HEADS].transpose(0, 2, 1)[:, :, None, :]
        attn_a = _fox_attention(z3, c, ct).reshape(M_TOK, WIDTH)
        attn_b = _chunk_attention(z3, bias_pieces[i]).reshape(M_TOK, WIDTH)
        mix = _fused_linear(
            "branch_mix", i, [hb, attn_a, attn_b],
            [(w_branch_gate, 0, 0), (w_branch_gate, 0, D_MODEL), (w_proj_a, 1, 0), (w_proj_b, 2, 0)],
            [], lambda d, e, rs: sig(d[0] * rs) * d[2] + sig(d[1] * rs) * d[3], BF16, D_MODEL,
            tm=512, tn=512, head_tm=512, head_tn=256, n_temps=6, row_ssq=ssq, out_buf=mix_buf,
            row_chunks=2)
        h, hb, ssq = _fused_linear("out_proj", i, [mix], [(w_out, 0, 0)], [h],
                                   lambda d, e, rs: e[0] + d[0], F32, D_MODEL,
                                   tm=1024, tn=512, head_tm=1024, head_tn=512, n_temps=2,
                                   next_gain=gain(ffn2_norm[i]), stats_bufs=(hb, ssq), row_chunks=4)

        h, hb, ssq, act_buf = _swiglu_ffn(i, hb, ssq, ffn2_w_gate, ffn2_w_up, ffn2_w_down, h,
                                          gain(ple_norm[i]), act_buf)

        last = i + 1 == DEPTH
        res = _fused_linear("ple", i, [hb, p_bf[i]], [(ple_w_gate, 0, 0), (ple_w_proj, 1, 0)], [h],
                            lambda d, e, rs: e[0] + sig(d[0] * rs) * d[1], F32, D_MODEL,
                            tm=1024, tn=512, head_tm=1024, head_tn=512, n_temps=4, row_ssq=ssq,
                            next_gain=None if last else gain(ffn1_norm[i + 1]),
                            stats_bufs=None if last else (mix, ssq_spare), row_chunks=4)
        if not last:
            mix_buf, ssq_spare = hb, ssq
        h, hb, ssq = (res, None, None) if last else res
    out = _rmsnorm(h, final_norm, F32)
    return out.reshape(BATCH, SEQ, D_MODEL)
```

```python
import functools
import math

import jax
import jax.numpy as jnp
import numpy as np
from jax import lax
from jax.experimental import pallas as pl
from jax.experimental.pallas import tpu as pltpu

D_MODEL = 4096
BATCH = 4
SEQ = 2048
DEPTH = 2
CHUNK = 64
PLE_DIM = 256
D_FF = 11008
HEAD_DIM = 128
N_HEADS = D_MODEL // (2 * HEAD_DIM)
WIDTH = N_HEADS * HEAD_DIM
LEFT_CHUNKS = 8
REL_CLIP = 128
RMS_EPS = 1e-6
N_QKV = 6 * WIDTH
M_TOK = BATCH * SEQ
LOG2E = math.log2(math.e)
Q_SCALE = HEAD_DIM ** -0.5 * LOG2E

F32 = jnp.float32
BF16 = jnp.bfloat16

LANES = 128
V7X_VMEM_REQUEST_CAP = 60000 * 1024
MIB = 1024 * 1024

_NT = (((1,), (1,)), ((), ()))
_TN = (((0,), (0,)), ((), ()))


def _nbytes(shape, dtype):
    return int(np.prod([s for s in shape if s is not None])) * jnp.dtype(dtype).itemsize


def _vmem_limit(blocks, single=(), temps=()):
    total = 2 * sum(_nbytes(s, d) for s, d in blocks)
    total += sum(_nbytes(s, d) for s, d in single)
    total += sum(_nbytes(s, d) for s, d in temps)
    total += 4 * MIB
    return min(total, V7X_VMEM_REQUEST_CAP)


def _params(semantics, limit):
    return pltpu.CompilerParams(dimension_semantics=semantics, vmem_limit_bytes=limit)


def _rmsnorm_kernel(h_ref, g_ref, o_ref):
    x = h_ref[...]
    ms = jnp.mean(x * x, axis=-1, keepdims=True)
    o_ref[...] = (x * lax.rsqrt(ms + RMS_EPS) * g_ref[...]).astype(o_ref.dtype)


def _rmsnorm(h, g, out_dtype, tr=512):
    m, d = h.shape
    blocks = [((tr, d), F32), ((1, d), F32), ((tr, d), out_dtype)]
    return pl.pallas_call(
        _rmsnorm_kernel,
        grid=(m // tr,),
        in_specs=[pl.BlockSpec((tr, d), lambda i: (i, 0)),
                  pl.BlockSpec((1, d), lambda i: (0, 0))],
        out_specs=pl.BlockSpec((tr, d), lambda i: (i, 0)),
        out_shape=jax.ShapeDtypeStruct((m, d), out_dtype),
        compiler_params=_params(("parallel",), _vmem_limit(blocks, temps=[((tr, d), F32)] * 2)),
        name="rmsnorm",
    )(h, g.reshape(1, d))


def _row_scale(ssq):
    return lax.rsqrt(ssq * (1.0 / D_MODEL) + RMS_EPS)


def _prenorm_kernel(h_ref, g_ref, h_out_ref, hb_ref, ssq_ref, spare_ref):
    x = h_ref[...]
    h_out_ref[...] = x
    hb_ref[...] = (x * g_ref[...]).astype(BF16)
    ssq = jnp.sum(x * x, axis=-1, keepdims=True)
    ssq_ref[...] = ssq
    spare_ref[...] = ssq


def _prenorm(h, g, tr=512):
    m, d = h.shape
    blocks = [((tr, d), F32), ((1, d), F32), ((tr, d), F32), ((tr, d), BF16)] + [((tr, LANES), F32)] * 2
    return pl.pallas_call(
        _prenorm_kernel,
        grid=(m // tr,),
        in_specs=[pl.BlockSpec((tr, d), lambda i: (i, 0)),
                  pl.BlockSpec((1, d), lambda i: (0, 0))],
        out_specs=[pl.BlockSpec((tr, d), lambda i: (i, 0)),
                   pl.BlockSpec((tr, d), lambda i: (i, 0)),
                   pl.BlockSpec((tr, 1), lambda i: (i, 0)),
                   pl.BlockSpec((tr, 1), lambda i: (i, 0))],
        out_shape=[jax.ShapeDtypeStruct((m, d), F32), jax.ShapeDtypeStruct((m, d), BF16),
                   jax.ShapeDtypeStruct((m, 1), F32), jax.ShapeDtypeStruct((m, 1), F32)],
        compiler_params=_params(("parallel",), _vmem_limit(blocks, temps=[((tr, d), F32)] * 2)),
        name="prenorm",
    )(h, g.reshape(1, d))


def _linear_kernel(*refs, w_act, n_acts, n_extras, epilogue, cast, scaled, stats, n_alias, col_axis,
                   w_rows_are_outputs, row_chunks, tile_rows):
    nw = len(w_act)
    it = iter(refs)
    take = lambda n: [next(it) for _ in range(n)]
    acts, ws, extras = take(n_acts), take(nw), take(n_extras)
    ssq_in = take(1)[0] if scaled else None
    gain = take(1)[0] if stats else None
    take(n_alias)
    out = take(1)[0]
    hb, ssq_out = take(2) if stats else (None, None)
    if cast:
        wbfs = take(nw)
        for w, wb in zip(ws, wbfs):
            wb[...] = w[...].astype(BF16)
        ws = wbfs
    dims = _NT if w_rows_are_outputs else (((1,), (0,)), ((), ()))
    rc = tile_rows // row_chunks
    fill = (out.shape[0] - tile_rows) // row_chunks
    parts = []
    for c in range(row_chunks):
        rows = slice(c * rc, (c + 1) * rc)
        if fill:
            out[tile_rows + c * fill:tile_rows + (c + 1) * fill, :] = jnp.zeros(
                (fill, out.shape[1]), out.dtype)
        dots = [lax.dot_general(acts[ai][rows, :], w[...], dims, preferred_element_type=F32)
                for ai, w in zip(w_act, ws)]
        rs = _row_scale(ssq_in[rows, :]) if scaled else None
        res = epilogue(dots, [e[rows, :] for e in extras], rs)
        out[rows, :] = res.astype(out.dtype)
        if stats:
            hb[rows, :] = (res * gain[...]).astype(BF16)
            parts.append(jnp.sum(res * res, axis=1, keepdims=True))
    if stats:
        part = jnp.concatenate(parts, axis=0) if row_chunks > 1 else parts[0]
        col = pl.program_id(col_axis)

        @pl.when(col == 0)
        def _():
            ssq_out[...] = part

        @pl.when(col != 0)
        def _():
            ssq_out[...] += part


def _fused_linear(name, layer, acts, weights, extras, epilogue, out_dtype, n_out, *,
                  tm, tn, head_tm, head_tn, n_temps, row_ssq=None, next_gain=None, stats_bufs=None,
                  out_buf=None, w_rows_are_outputs=False, row_chunks=1):
    m = acts[0].shape[0]
    w_act = tuple(ai for _, ai, _ in weights)
    wt = w_rows_are_outputs
    ks = [w.shape[2 if wt else 1] for w, _, _ in weights]
    scaled, stats = row_ssq is not None, next_gain is not None
    in_place = len(extras) == 1
    assert len(extras) <= 1 and (in_place or not stats) and stats == (stats_bufs is not None)
    assert out_buf is None or not in_place
    kern = functools.partial(_linear_kernel, w_act=w_act, n_acts=len(acts), n_extras=len(extras),
                             epilogue=epilogue, scaled=scaled, stats=stats, w_rows_are_outputs=wt,
                             row_chunks=row_chunks)
    w_blk = lambda k, n: (n, k) if wt else (k, n)
    w_idx = lambda c: (c, 0) if wt else (0, c)
    side_in = ([row_ssq] if scaled else []) + ([next_gain] if stats else [])

    def specs(tm_, tn_, row, colblk):
        ins = [pl.BlockSpec((tm_, tn_), lambda *g: (row(*g), colblk(*g))) for _ in extras]
        ins += [pl.BlockSpec((tm_, 1), lambda *g: (row(*g), 0))] if scaled else []
        ins += [pl.BlockSpec((1, tn_), lambda *g: (0, colblk(*g)))] if stats else []
        outs = [pl.BlockSpec((tm_, tn_), lambda *g: (row(*g), colblk(*g)))]
        shapes = [jax.ShapeDtypeStruct((m, n_out), out_dtype)]
        blocks = [((tm_, tn_), F32)] * len(extras) + [((tm_, tn_), out_dtype)]
        if stats:
            outs += [pl.BlockSpec((tm_, tn_), lambda *g: (row(*g), colblk(*g))),
                     pl.BlockSpec((tm_, 1), lambda *g: (row(*g), 0))]
            shapes += [jax.ShapeDtypeStruct((m, n_out), BF16), jax.ShapeDtypeStruct((m, 1), F32)]
            blocks += [((tm_, tn_), BF16), ((tm_, LANES), F32)]
        blocks += [((tm_, LANES), F32)] if scaled else []
        return ins, outs, shapes, blocks

    for e in extras:
        assert e.shape[1] == n_out
    ins, outs, shapes, blocks = specs(head_tm, head_tn, lambda j: 0, lambda j: j)
    in_specs = [pl.BlockSpec((head_tm, a.shape[1]), lambda j: (0, 0), pipeline_mode=pl.Buffered(1))
                for a in acts]
    in_specs += [pl.BlockSpec((None,) + w_blk(k, head_tn),
                              lambda j, off=c0 // head_tn: (layer,) + w_idx(off + j))
                 for k, (_, _, c0) in zip(ks, weights)]
    n_in = len(in_specs) + len(ins)
    if in_place:
        aliases = {len(in_specs): 0}
        aliases.update({n_in + r: 1 + r for r in range(2 if stats else 0)})
    elif out_buf is not None:
        aliases = {n_in: 0}
    else:
        aliases = {}
        outs[0] = pl.BlockSpec((m, head_tn), lambda j: (0, j))
        blocks += [((m - head_tm, head_tn), out_dtype)]
    outs += [pl.BlockSpec(w_blk(k, head_tn), lambda j: w_idx(j)) for k in ks]
    shapes += [jax.ShapeDtypeStruct(w_blk(k, n_out), BF16) for k in ks]
    blocks += [((k, head_tn), F32) for k in ks] + [((k, head_tn), BF16) for k in ks]
    bufs = list(stats_bufs) if stats else [] if out_buf is None else [out_buf]
    res = pl.pallas_call(
        functools.partial(kern, cast=True, n_alias=len(bufs), col_axis=0, tile_rows=head_tm),
        grid=(n_out // head_tn,),
        in_specs=in_specs + ins + [pl.BlockSpec(memory_space=pl.ANY)] * len(bufs),
        out_specs=outs, out_shape=shapes, input_output_aliases=aliases,
        compiler_params=_params(("arbitrary",),
                                _vmem_limit(blocks, single=[((head_tm, a.shape[1]), BF16) for a in acts],
                                            temps=[((head_tm, head_tn), F32)] * n_temps)),
        name=name + "_head",
    )(*acts, *[w for w, _, _ in weights], *extras, *side_in, *bufs)
    n_res = 3 if stats else 1
    prior, wbfs = res[:n_res], res[n_res:]

    r0 = head_tm // tm
    ins, outs, shapes, blocks = specs(tm, tn, lambda i, j: i + r0, lambda i, j: j)
    in_specs = [pl.BlockSpec((tm, a.shape[1]), lambda i, j: (i + r0, 0)) for a in acts]
    in_specs += [pl.BlockSpec(w_blk(k, tn), lambda i, j: w_idx(j)) for k in ks]
    n_in = len(in_specs) + len(ins)
    blocks += [((tm, a.shape[1]), BF16) for a in acts] + [((k, tn), BF16) for k in ks]
    if in_place:
        residual, unread = [prior[0]], list(prior[1:])
        aliases = {len(in_specs): 0}
        aliases.update({n_in + r: 1 + r for r in range(len(unread))})
    else:
        residual, unread = [], list(prior)
        aliases = {n_in: 0}
    res = pl.pallas_call(
        functools.partial(kern, cast=False, n_alias=len(unread), col_axis=1, tile_rows=tm),
        grid=(m // tm - r0, n_out // tn),
        in_specs=in_specs + ins + [pl.BlockSpec(memory_space=pl.ANY)] * len(unread),
        out_specs=outs, out_shape=shapes, input_output_aliases=aliases,
        compiler_params=_params(("parallel", "arbitrary"),
                                _vmem_limit(blocks, temps=[((tm, tn), F32)] * n_temps)),
        name=name + "_tail",
    )(*acts, *wbfs, *residual, *side_in, *unread)
    return tuple(res) if stats else res[0]


def _ffn_down(layer, act, w_down, h, next_gain, stats_bufs, *, tm=512, tn=512, head_tm=1024,
              head_tn=256, row_chunks=4):
    m, k = act.shape
    d = w_down.shape[2]
    kh = k // 2
    kern = functools.partial(_linear_kernel, w_rows_are_outputs=False, scaled=False, row_chunks=row_chunks)
    head_act = lambda part: pl.BlockSpec((head_tm, kh), lambda j: (0, part), pipeline_mode=pl.Buffered(1))
    head_w = lambda part: pl.BlockSpec((None, kh, head_tn), lambda j: (layer, part, j))
    tile = pl.BlockSpec((head_tm, head_tn), lambda j: (0, j))
    w_out = pl.BlockSpec((kh, head_tn), lambda j: (0, j))
    w_blocks = [((kh, head_tn), F32), ((kh, head_tn), BF16)]
    single = [((head_tm, kh), BF16)]
    temps = [((head_tm, head_tn), F32)] * 2

    partial, wbf_a = pl.pallas_call(
        functools.partial(kern, w_act=(0,), n_acts=1, n_extras=0, epilogue=lambda dd, e, rs: dd[0],
                          cast=True, stats=False, n_alias=0, col_axis=0, tile_rows=head_tm),
        grid=(d // head_tn,),
        in_specs=[head_act(0), head_w(0)],
        out_specs=[tile, w_out],
        out_shape=[jax.ShapeDtypeStruct((head_tm, d), F32), jax.ShapeDtypeStruct((kh, d), BF16)],
        compiler_params=_params(("arbitrary",),
                                _vmem_limit(w_blocks + [((head_tm, head_tn), F32)], single=single, temps=temps)),
        name="ffn_down_head_a",
    )(act, w_down)

    stat_specs = [tile, pl.BlockSpec((head_tm, 1), lambda j: (0, 0))]
    h_new, hb, ssq, wbf_b = pl.pallas_call(
        functools.partial(kern, w_act=(0,), n_acts=1, n_extras=2,
                          epilogue=lambda dd, e, rs: e[0] + 0.5 * (e[1] + dd[0]),
                          cast=True, stats=True, n_alias=2, col_axis=0, tile_rows=head_tm),
        grid=(d // head_tn,),
        in_specs=[head_act(1), head_w(1), tile, tile, pl.BlockSpec((1, head_tn), lambda j: (0, j)),
                  pl.BlockSpec(memory_space=pl.ANY), pl.BlockSpec(memory_space=pl.ANY)],
        out_specs=[tile] + stat_specs + [w_out],
        out_shape=[jax.ShapeDtypeStruct((m, d), F32), jax.ShapeDtypeStruct((m, d), BF16),
                   jax.ShapeDtypeStruct((m, 1), F32), jax.ShapeDtypeStruct((kh, d), BF16)],
        input_output_aliases={2: 0, 5: 1, 6: 2},
        compiler_params=_params(("arbitrary",), _vmem_limit(
            w_blocks + [((head_tm, head_tn), F32)] * 3 + [((head_tm, head_tn), BF16), ((head_tm, LANES), F32)],
            single=single, temps=temps)),
        name="ffn_down_head_b",
    )(act, w_down, h, partial, next_gain, *stats_bufs)

    r0 = head_tm // tm
    tile = pl.BlockSpec((tm, tn), lambda i, j: (i + r0, j))
    acts_t = [pl.BlockSpec((tm, kh), lambda i, j, part=part: (i + r0, part)) for part in range(2)]
    w_t = pl.BlockSpec((kh, tn), lambda i, j: (0, j))
    return pl.pallas_call(
        functools.partial(kern, w_act=(0, 1), n_acts=2, n_extras=1,
                          epilogue=lambda dd, e, rs: e[0] + 0.5 * (dd[0] + dd[1]),
                          cast=False, stats=True, n_alias=2, col_axis=1, tile_rows=tm),
        grid=(m // tm - r0, d // tn),
        in_specs=acts_t + [w_t, w_t, tile, pl.BlockSpec((1, tn), lambda i, j: (0, j)),
                           pl.BlockSpec(memory_space=pl.ANY), pl.BlockSpec(memory_space=pl.ANY)],
        out_specs=[tile, tile, pl.BlockSpec((tm, 1), lambda i, j: (i + r0, 0))],
        out_shape=[jax.ShapeDtypeStruct((m, d), F32), jax.ShapeDtypeStruct((m, d), BF16),
                   jax.ShapeDtypeStruct((m, 1), F32)],
        input_output_aliases={4: 0, 6: 1, 7: 2},
        compiler_params=_params(("parallel", "arbitrary"), _vmem_limit(
            [((tm, kh), BF16), ((kh, tn), BF16)] * 2 + [((tm, tn), F32)] * 2
            + [((tm, tn), BF16), ((tm, LANES), F32)], temps=[((tm, tn), F32)] * 2)),
        name="ffn_down_tail",
    )(act, act, wbf_a, wbf_b, h_new, next_gain, hb, ssq)


def _swiglu_ffn(layer, hb, ssq, w_gate, w_up, w_down, h, next_gain, act_buf):
    def gate_up(d, e, rs):
        g = d[0] * rs
        return g * jax.nn.sigmoid(g) * (d[1] * rs)

    act = _fused_linear("ffn_gateup", layer, [hb], [(w_gate, 0, 0), (w_up, 0, 0)], [], gate_up,
                        BF16, D_FF, tm=2048, tn=256, head_tm=2048, head_tn=256, n_temps=4,
                        row_ssq=ssq, out_buf=act_buf, row_chunks=8)
    return tuple(_ffn_down(layer, act, w_down, h, next_gain, (hb, ssq))) + (act,)


def _log_sigmoid(x):
    return jnp.minimum(x, 0.0) - jnp.log1p(jnp.exp(-jnp.abs(x)))


def _forget_kernel(u_ref, ssq_ref, w_ref, b_ref, c_ref, carry_ref, *, ts):
    @pl.when(pl.program_id(1) == 0)
    def _():
        carry_ref[...] = jnp.zeros_like(carry_ref)

    w = w_ref[...].astype(BF16)
    w = jnp.concatenate([w, jnp.zeros((LANES - N_HEADS, w.shape[1]), BF16)], axis=0)
    logit = lax.dot_general(u_ref[0], w, _NT, preferred_element_type=F32)
    logit = logit * _row_scale(ssq_ref[0]) + b_ref[...]
    c = _log_sigmoid(logit)
    row = lax.broadcasted_iota(jnp.int32, c.shape, 0)
    shift = 1
    while shift < ts:
        c = c + jnp.where(row >= shift, pltpu.roll(c, shift, axis=0), 0.0)
        shift *= 2
    c = c + carry_ref[...]
    c_ref[0] = c
    carry_ref[...] = c[ts - 1:ts, :]


def _forget_cumsum(layer, u3, ssq3, w_in_t, bias, ts=512):
    b, s, d = u3.shape
    blocks = [((1, ts, d), BF16), ((1, ts, LANES), F32), ((N_HEADS, d), F32), ((1, LANES), F32),
              ((1, ts, LANES), F32)]
    return pl.pallas_call(
        functools.partial(_forget_kernel, ts=ts),
        grid=(b, s // ts),
        in_specs=[pl.BlockSpec((1, ts, d), lambda bi, si: (bi, si, 0)),
                  pl.BlockSpec((1, ts, 1), lambda bi, si: (bi, si, 0)),
                  pl.BlockSpec((None, N_HEADS, d), lambda bi, si: (layer, N_QKV // N_HEADS, 0)),
                  pl.BlockSpec((1, LANES), lambda bi, si: (0, 0))],
        out_specs=pl.BlockSpec((1, ts, LANES), lambda bi, si: (bi, si, 0)),
        out_shape=jax.ShapeDtypeStruct((b, s, LANES), F32),
        scratch_shapes=[pltpu.VMEM((1, LANES), F32)],
        compiler_params=_params(("parallel", "arbitrary"),
                                _vmem_limit(blocks, temps=[((ts, LANES), F32)] * 8 + [((LANES, d), BF16)])),
        name="forget_cumsum",
    )(u3, ssq3, w_in_t, bias)


_HPS = 4
_HW = _HPS * HEAD_DIM


def _head_cols(hh):
    return slice(hh * HEAD_DIM, (hh + 1) * HEAD_DIM)


def _fox_kernel(q_ref, k_ref, v_ref, c_ref, ct_ref, o_ref, *, tq):
    hp = pl.program_id(1)
    i = pl.program_id(2)
    q_start = pl.multiple_of(i * tq, tq)
    half = tq // 2
    q = [(q_ref[0, :, _head_cols(hh)].astype(F32) * Q_SCALE).astype(BF16)
         for hh in range(_HPS)]
    c_t = [ct_ref[0, hh, :, pl.ds(q_start, tq)] * LOG2E for hh in range(_HPS)]

    def block(start, nk, q_lo, carry, hh, diag_offset=None):
        m, l, acc = carry
        nq = tq - q_lo
        k = k_ref[0, pl.ds(start, nk), _head_cols(hh)]
        v = v_ref[0, pl.ds(start, nk), _head_cols(hh)]
        lane = lax.broadcasted_iota(jnp.int32, (nk, LANES), 1)
        c_s = jnp.sum(jnp.where(lane == hp * _HPS + hh, c_ref[0, pl.ds(start, nk), :], 0.0),
                      axis=1, keepdims=True)
        x = lax.dot_general(k, q[hh][q_lo:, :], _NT, preferred_element_type=F32) - c_s * LOG2E
        if diag_offset is not None:
            key = lax.broadcasted_iota(jnp.int32, (nk, nq), 0) + diag_offset
            qry = lax.broadcasted_iota(jnp.int32, (nk, nq), 1) + q_lo
            x = jnp.where(key <= qry, x, -jnp.inf)
        ct = c_t[hh][:, q_lo:]
        m_new = jnp.maximum(m, jnp.max(x, axis=0, keepdims=True) + ct)
        p = jnp.exp2(x + (ct - m_new))
        alpha = jnp.exp2(m - m_new)
        l = alpha * l + jnp.sum(p, axis=0, keepdims=True)
        acc = alpha * acc + lax.dot_general(v, p.astype(BF16), _TN, preferred_element_type=F32)
        return m_new, l, acc

    def full_blocks(j, carries):
        start = pl.multiple_of(j * tq, tq)
        return tuple(block(start, tq, 0, carries[hh], hh) for hh in range(_HPS))

    init = (jnp.full((1, tq), -jnp.inf, F32), jnp.zeros((1, tq), F32),
            jnp.zeros((HEAD_DIM, tq), F32))
    carries = lax.fori_loop(0, i, full_blocks, (init,) * _HPS)
    for hh in range(_HPS):
        carry = block(q_start, half, 0, carries[hh], hh, diag_offset=0)
        late = block(pl.multiple_of(q_start + half, half), half, half,
                     tuple(t[:, half:] for t in carry), hh, diag_offset=half)
        _, l, acc = (jnp.concatenate([t[:, :half], u], axis=1) for t, u in zip(carry, late))
        o_ref[0, :, _head_cols(hh)] = (acc / l).T.astype(o_ref.dtype)


def _fox_attention(z3, c, ct, tq=1024):
    b, s, _ = z3.shape
    npair = N_HEADS // _HPS
    blocks = [((1, tq, _HW), BF16), ((1, s, _HW), BF16), ((1, s, _HW), BF16),
              ((1, s, LANES), F32), ((1, _HPS, 1, s), F32), ((1, tq, _HW), BF16)]
    return pl.pallas_call(
        functools.partial(_fox_kernel, tq=tq),
        grid=(b, npair, s // tq),
        in_specs=[pl.BlockSpec((1, tq, _HW), lambda bi, hp, i: (bi, i, hp)),
                  pl.BlockSpec((1, s, _HW), lambda bi, hp, i: (bi, 0, npair + hp)),
                  pl.BlockSpec((1, s, _HW), lambda bi, hp, i: (bi, 0, 2 * npair + hp)),
                  pl.BlockSpec((1, s, LANES), lambda bi, hp, i: (bi, 0, 0)),
                  pl.BlockSpec((1, _HPS, 1, s), lambda bi, hp, i: (bi, hp, 0, 0))],
        out_specs=pl.BlockSpec((1, tq, _HW), lambda bi, hp, i: (bi, i, hp)),
        out_shape=jax.ShapeDtypeStruct((b, s, WIDTH), BF16),
        compiler_params=_params(("parallel", "parallel", "arbitrary"),
                                _vmem_limit(blocks, temps=[((tq, tq), F32)] * 6 * _HPS)),
        name="fox_attention",
    )(z3, z3, z3, c, ct)


_QB = 8 * CHUNK
_N_BIAS_VEC = 12


def _bias_pieces_index():
    e = 128 * (np.arange(_N_BIAS_VEC)[:, None] - 4) + np.arange(LANES)[None, :]
    dist = _QB - e
    return np.clip(dist, -REL_CLIP, REL_CLIP) + REL_CLIP


def _chunk_kernel(q_ref, kp_ref, kc_ref, vp_ref, vc_ref, g_ref, o_ref, bias_ref):
    bi = pl.program_id(1)
    i = pl.program_id(2)

    @pl.when((bi == 0) & (i == 0))
    def _():
        r = lax.broadcasted_iota(jnp.int32, (LANES, LANES), 0)
        col = lax.broadcasted_iota(jnp.int32, (LANES, LANES), 1)
        upper = col >= r
        for hh in range(_HPS):
            rolled = [pltpu.roll(jnp.broadcast_to(g_ref[hh, k:k + 1, :] * LOG2E, (LANES, LANES)),
                                 0, 1, stride=1, stride_axis=0) for k in range(_N_BIAS_VEC)]
            for rb in range(_QB // LANES):
                for cb in range(2 * _QB // LANES):
                    delta = cb - rb + 4
                    tile = jnp.where(upper, rolled[delta], rolled[delta - 1])
                    q_chunk = (rb * LANES + r) // CHUNK
                    k_chunk = (cb * LANES + col) // CHUNK
                    ok = (k_chunk >= q_chunk) & (k_chunk <= q_chunk + LEFT_CHUNKS)
                    bias_ref[hh, cb * LANES:(cb + 1) * LANES, rb * LANES:(rb + 1) * LANES] = (
                        jnp.where(ok, tile, -jnp.inf).T)

    def attend(with_prev):
        for hh in range(_HPS):
            cols = _head_cols(hh)
            q = (q_ref[0, :, cols].astype(F32) * Q_SCALE).astype(BF16)
            x_cur = (lax.dot_general(kc_ref[0, :, cols], q, _NT, preferred_element_type=F32)
                     + bias_ref[hh, _QB:, :])
            m = jnp.max(x_cur, axis=0, keepdims=True)
            if with_prev:
                x_prev = (lax.dot_general(kp_ref[0, :, cols], q, _NT, preferred_element_type=F32)
                          + bias_ref[hh, :_QB, :])
                m = jnp.maximum(m, jnp.max(x_prev, axis=0, keepdims=True))
            p_cur = jnp.exp2(x_cur - m)
            l = jnp.sum(p_cur, axis=0, keepdims=True)
            acc = lax.dot_general(vc_ref[0, :, cols], p_cur.astype(BF16), _TN,
                                  preferred_element_type=F32)
            if with_prev:
                p_prev = jnp.exp2(x_prev - m)
                l = l + jnp.sum(p_prev, axis=0, keepdims=True)
                acc = acc + lax.dot_general(vp_ref[0, :, cols], p_prev.astype(BF16), _TN,
                                            preferred_element_type=F32)
            o_ref[0, :, cols] = (acc / l).T.astype(o_ref.dtype)

    @pl.when(i == 0)
    def _():
        attend(False)

    @pl.when(i > 0)
    def _():
        attend(True)


def _chunk_attention(z3, bias_pieces):
    b, s, _ = z3.shape
    nb = s // _QB
    npair = N_HEADS // _HPS
    qo, ko, vo = 3 * npair, 4 * npair, 5 * npair
    blk = (1, _QB, _HW)
    blocks = [(blk, BF16)] * 6 + [((_HPS, 16, LANES), F32)]
    prev = lambda i: jnp.maximum(i - 1, 0)
    return pl.pallas_call(
        _chunk_kernel,
        grid=(npair, b, nb),
        in_specs=[pl.BlockSpec(blk, lambda hp, bi, i: (bi, i, qo + hp)),
                  pl.BlockSpec(blk, lambda hp, bi, i: (bi, prev(i), ko + hp)),
                  pl.BlockSpec(blk, lambda hp, bi, i: (bi, i, ko + hp)),
                  pl.BlockSpec(blk, lambda hp, bi, i: (bi, prev(i), vo + hp)),
                  pl.BlockSpec(blk, lambda hp, bi, i: (bi, i, vo + hp)),
                  pl.BlockSpec((_HPS, 16, LANES), lambda hp, bi, i: (hp, 0, 0))],
        out_specs=pl.BlockSpec(blk, lambda hp, bi, i: (bi, i, hp)),
        out_shape=jax.ShapeDtypeStruct((b, s, WIDTH), BF16),
        scratch_shapes=[pltpu.VMEM((_HPS, 2 * _QB, _QB), F32)],
        compiler_params=_params(("arbitrary", "arbitrary", "arbitrary"),
                                _vmem_limit(blocks, single=[((_HPS, 2 * _QB, _QB), F32)],
                                            temps=[((_QB, _QB), F32)] * 8 * _HPS)),
        name="chunk_attention",
    )(z3, z3, z3, z3, z3, bias_pieces)


def kernel(x, p, ffn1_norm, ffn1_w_gate, ffn1_w_up, ffn1_w_down, mix_norm, w_in, fox_forget_bias,
           rel_bias, w_branch_gate, w_proj_a, w_proj_b, w_out, ffn2_norm, ffn2_w_gate, ffn2_w_up,
           ffn2_w_down, ple_norm, ple_w_gate, ple_w_proj, final_norm):
    forget_bias = jnp.pad(fox_forget_bias, ((0, 0), (0, LANES - N_HEADS)))[:, None, :]
    p_bf = p.astype(BF16).reshape(DEPTH, M_TOK, PLE_DIM)
    bias_pieces = jnp.pad(rel_bias[:, :, _bias_pieces_index()],
                          ((0, 0), (0, 0), (0, 16 - _N_BIAS_VEC), (0, 0)))
    sig = jax.nn.sigmoid
    gain = lambda g: g.reshape(1, D_MODEL)
    w_in_t = jnp.swapaxes(w_in, 1, 2)

    h, hb, ssq, ssq_spare = _prenorm(x.reshape(M_TOK, D_MODEL), ffn1_norm[0])
    act_buf = z_buf = mix_buf = None
    for i in range(DEPTH):
        h, hb, ssq, act_buf = _swiglu_ffn(i, hb, ssq, ffn1_w_gate, ffn1_w_up, ffn1_w_down, h,
                                          gain(mix_norm[i]), act_buf)

        z = _fused_linear("qkv_proj", i, [hb], [(w_in_t, 0, 0)], [], lambda d, e, rs: d[0] * rs,
                          BF16, N_QKV, tm=1024, tn=1024, head_tm=1024, head_tn=512, n_temps=2,
                          row_ssq=ssq, out_buf=z_buf, w_rows_are_outputs=True, row_chunks=4)
        z3, z_buf = z.reshape(BATCH, SEQ, N_QKV), z
        c = _forget_cumsum(i, hb.reshape(BATCH, SEQ, D_MODEL), ssq.reshape(BATCH, SEQ, 1), w_in_t,
                           forget_bias[i])
        ct = c[:, :, :N_HEADS].transpose(0, 2, 1)[:, :, None, :]
        attn_a = _fox_attention(z3, c, ct).reshape(M_TOK, WIDTH)
        attn_b = _chunk_attention(z3, bias_pieces[i]).reshape(M_TOK, WIDTH)
        mix = _fused_linear(
            "branch_mix", i, [hb, attn_a, attn_b],
            [(w_branch_gate, 0, 0), (w_branch_gate, 0, D_MODEL), (w_proj_a, 1, 0), (w_proj_b, 2, 0)],
            [], lambda d, e, rs: sig(d[0] * rs) * d[2] + sig(d[1] * rs) * d[3], BF16, D_MODEL,
            tm=512, tn=512, head_tm=512, head_tn=256, n_temps=6, row_ssq=ssq, out_buf=mix_buf,
            row_chunks=2)
        h, hb, ssq = _fused_linear("out_proj", i, [mix], [(w_out, 0, 0)], [h],
                                   lambda d, e, rs: e[0] + d[0], F32, D_MODEL,
                                   tm=1024, tn=512, head_tm=1024, head_tn=512, n_temps=2,
                                   next_gain=gain(ffn2_norm[i]), stats_bufs=(hb, ssq), row_chunks=4)

        h, hb, ssq, act_buf = _swiglu_ffn(i, hb, ssq, ffn2_w_gate, ffn2_w_up, ffn2_w_down, h,
                                          gain(ple_norm[i]), act_buf)

        last = i + 1 == DEPTH
        res = _fused_linear("ple", i, [hb, p_bf[i]], [(ple_w_gate, 0, 0), (ple_w_proj, 1, 0)], [h],
                            lambda d, e, rs: e[0] + sig(d[0] * rs) * d[1], F32, D_MODEL,
                            tm=1024, tn=512, head_tm=1024, head_tn=512, n_temps=4, row_ssq=ssq,
                            next_gain=None if last else gain(ffn1_norm[i + 1]),
                            stats_bufs=None if last else (mix, ssq_spare), row_chunks=4)
        if not last:
            mix_buf, ssq_spare = hb, ssq
        h, hb, ssq = (res, None, None) if last else res
    out = _rmsnorm(h, final_norm, F32)
    return out.reshape(BATCH, SEQ, D_MODEL)
```

```python
import functools
import math

import jax
import jax.numpy as jnp
import numpy as np
from jax import lax
from jax.experimental import pallas as pl
from jax.experimental.pallas import tpu as pltpu

D_MODEL = 4096
BATCH = 4
SEQ = 2048
DEPTH = 2
CHUNK = 64
PLE_DIM = 256
D_FF = 11008
HEAD_DIM = 128
N_HEADS = D_MODEL // (2 * HEAD_DIM)
WIDTH = N_HEADS * HEAD_DIM
LEFT_CHUNKS = 8
REL_CLIP = 128
RMS_EPS = 1e-6
N_QKV = 6 * WIDTH
M_TOK = BATCH * SEQ
LOG2E = math.log2(math.e)
Q_SCALE = HEAD_DIM ** -0.5 * LOG2E

F32 = jnp.float32
BF16 = jnp.bfloat16

LANES = 128
V7X_VMEM_REQUEST_CAP = 60000 * 1024
MIB = 1024 * 1024

_NT = (((1,), (1,)), ((), ()))
_TN = (((0,), (0,)), ((), ()))


def _nbytes(shape, dtype):
    return int(np.prod([s for s in shape if s is not None])) * jnp.dtype(dtype).itemsize


def _vmem_limit(blocks, single=(), temps=()):
    total = 2 * sum(_nbytes(s, d) for s, d in blocks)
    total += sum(_nbytes(s, d) for s, d in single)
    total += sum(_nbytes(s, d) for s, d in temps)
    total += 4 * MIB
    return min(total, V7X_VMEM_REQUEST_CAP)


def _params(semantics, limit):
    return pltpu.CompilerParams(dimension_semantics=semantics, vmem_limit_bytes=limit)


def _rmsnorm_kernel(h_ref, g_ref, o_ref):
    x = h_ref[...]
    ms = jnp.mean(x * x, axis=-1, keepdims=True)
    o_ref[...] = (x * lax.rsqrt(ms + RMS_EPS) * g_ref[...]).astype(o_ref.dtype)


def _rmsnorm(h, g, out_dtype, tr=512):
    m, d = h.shape
    blocks = [((tr, d), F32), ((1, d), F32), ((tr, d), out_dtype)]
    return pl.pallas_call(
        _rmsnorm_kernel,
        grid=(m // tr,),
        in_specs=[pl.BlockSpec((tr, d), lambda i: (i, 0)),
                  pl.BlockSpec((1, d), lambda i: (0, 0))],
        out_specs=pl.BlockSpec((tr, d), lambda i: (i, 0)),
        out_shape=jax.ShapeDtypeStruct((m, d), out_dtype),
        compiler_params=_params(("parallel",), _vmem_limit(blocks, temps=[((tr, d), F32)] * 2)),
        name="rmsnorm",
    )(h, g.reshape(1, d))


def _row_scale(ssq):
    return lax.rsqrt(ssq * (1.0 / D_MODEL) + RMS_EPS)


def _prenorm_kernel(h_ref, g_ref, h_out_ref, hb_ref, ssq_ref, spare_ref):
    x = h_ref[...]
    h_out_ref[...] = x
    hb_ref[...] = (x * g_ref[...]).astype(BF16)
    ssq = jnp.sum(x * x, axis=-1, keepdims=True)
    ssq_ref[...] = ssq
    spare_ref[...] = ssq


def _prenorm(h, g, tr=512):
    m, d = h.shape
    blocks = [((tr, d), F32), ((1, d), F32), ((tr, d), F32), ((tr, d), BF16)] + [((tr, LANES), F32)] * 2
    return pl.pallas_call(
        _prenorm_kernel,
        grid=(m // tr,),
        in_specs=[pl.BlockSpec((tr, d), lambda i: (i, 0)),
                  pl.BlockSpec((1, d), lambda i: (0, 0))],
        out_specs=[pl.BlockSpec((tr, d), lambda i: (i, 0)),
                   pl.BlockSpec((tr, d), lambda i: (i, 0)),
                   pl.BlockSpec((tr, 1), lambda i: (i, 0)),
                   pl.BlockSpec((tr, 1), lambda i: (i, 0))],
        out_shape=[jax.ShapeDtypeStruct((m, d), F32), jax.ShapeDtypeStruct((m, d), BF16),
                   jax.ShapeDtypeStruct((m, 1), F32), jax.ShapeDtypeStruct((m, 1), F32)],
        compiler_params=_params(("parallel",), _vmem_limit(blocks, temps=[((tr, d), F32)] * 2)),
        name="prenorm",
    )(h, g.reshape(1, d))


def _linear_kernel(*refs, w_act, n_acts, n_extras, epilogue, cast, scaled, stats, n_alias, col_axis,
                   w_rows_are_outputs, row_chunks, tile_rows):
    nw = len(w_act)
    it = iter(refs)
    take = lambda n: [next(it) for _ in range(n)]
    acts, ws, extras = take(n_acts), take(nw), take(n_extras)
    ssq_in = take(1)[0] if scaled else None
    gain = take(1)[0] if stats else None
    take(n_alias)
    out = take(1)[0]
    hb, ssq_out = take(2) if stats else (None, None)
    if cast:
        wbfs = take(nw)
        for w, wb in zip(ws, wbfs):
            wb[...] = w[...].astype(BF16)
        ws = wbfs
    dims = _NT if w_rows_are_outputs else (((1,), (0,)), ((), ()))
    rc = tile_rows // row_chunks
    fill = (out.shape[0] - tile_rows) // row_chunks
    parts = []
    for c in range(row_chunks):
        rows = slice(c * rc, (c + 1) * rc)
        if fill:
            out[tile_rows + c * fill:tile_rows + (c + 1) * fill, :] = jnp.zeros(
                (fill, out.shape[1]), out.dtype)
        dots = [lax.dot_general(acts[ai][rows, :], w[...], dims, preferred_element_type=F32)
                for ai, w in zip(w_act, ws)]
        rs = _row_scale(ssq_in[rows, :]) if scaled else None
        res = epilogue(dots, [e[rows, :] for e in extras], rs)
        out[rows, :] = res.astype(out.dtype)
        if stats:
            hb[rows, :] = (res * gain[...]).astype(BF16)
            parts.append(jnp.sum(res * res, axis=1, keepdims=True))
    if stats:
        part = jnp.concatenate(parts, axis=0) if row_chunks > 1 else parts[0]
        col = pl.program_id(col_axis)

        @pl.when(col == 0)
        def _():
            ssq_out[...] = part

        @pl.when(col != 0)
        def _():
            ssq_out[...] += part


def _fused_linear(name, layer, acts, weights, extras, epilogue, out_dtype, n_out, *,
                  tm, tn, head_tm, head_tn, n_temps, row_ssq=None, next_gain=None, stats_bufs=None,
                  out_buf=None, w_rows_are_outputs=False, row_chunks=1):
    m = acts[0].shape[0]
    w_act = tuple(ai for _, ai, _ in weights)
    wt = w_rows_are_outputs
    ks = [w.shape[2 if wt else 1] for w, _, _ in weights]
    scaled, stats = row_ssq is not None, next_gain is not None
    in_place = len(extras) == 1
    assert len(extras) <= 1 and (in_place or not stats) and stats == (stats_bufs is not None)
    assert out_buf is None or not in_place
    kern = functools.partial(_linear_kernel, w_act=w_act, n_acts=len(acts), n_extras=len(extras),
                             epilogue=epilogue, scaled=scaled, stats=stats, w_rows_are_outputs=wt,
                             row_chunks=row_chunks)
    w_blk = lambda k, n: (n, k) if wt else (k, n)
    w_idx = lambda c: (c, 0) if wt else (0, c)
    side_in = ([row_ssq] if scaled else []) + ([next_gain] if stats else [])

    def specs(tm_, tn_, row, colblk):
        ins = [pl.BlockSpec((tm_, tn_), lambda *g: (row(*g), colblk(*g))) for _ in extras]
        ins += [pl.BlockSpec((tm_, 1), lambda *g: (row(*g), 0))] if scaled else []
        ins += [pl.BlockSpec((1, tn_), lambda *g: (0, colblk(*g)))] if stats else []
        outs = [pl.BlockSpec((tm_, tn_), lambda *g: (row(*g), colblk(*g)))]
        shapes = [jax.ShapeDtypeStruct((m, n_out), out_dtype)]
        blocks = [((tm_, tn_), F32)] * len(extras) + [((tm_, tn_), out_dtype)]
        if stats:
            outs += [pl.BlockSpec((tm_, tn_), lambda *g: (row(*g), colblk(*g))),
                     pl.BlockSpec((tm_, 1), lambda *g: (row(*g), 0))]
            shapes += [jax.ShapeDtypeStruct((m, n_out), BF16), jax.ShapeDtypeStruct((m, 1), F32)]
            blocks += [((tm_, tn_), BF16), ((tm_, LANES), F32)]
        blocks += [((tm_, LANES), F32)] if scaled else []
        return ins, outs, shapes, blocks

    for e in extras:
        assert e.shape[1] == n_out
    ins, outs, shapes, blocks = specs(head_tm, head_tn, lambda j: 0, lambda j: j)
    in_specs = [pl.BlockSpec((head_tm, a.shape[1]), lambda j: (0, 0), pipeline_mode=pl.Buffered(1))
                for a in acts]
    in_specs += [pl.BlockSpec((None,) + w_blk(k, head_tn),
                              lambda j, off=c0 // head_tn: (layer,) + w_idx(off + j))
                 for k, (_, _, c0) in zip(ks, weights)]
    n_in = len(in_specs) + len(ins)
    if in_place:
        aliases = {len(in_specs): 0}
        aliases.update({n_in + r: 1 + r for r in range(2 if stats else 0)})
    elif out_buf is not None:
        aliases = {n_in: 0}
    else:
        aliases = {}
        outs[0] = pl.BlockSpec((m, head_tn), lambda j: (0, j))
        blocks += [((m - head_tm, head_tn), out_dtype)]
    outs += [pl.BlockSpec(w_blk(k, head_tn), lambda j: w_idx(j)) for k in ks]
    shapes += [jax.ShapeDtypeStruct(w_blk(k, n_out), BF16) for k in ks]
    blocks += [((k, head_tn), F32) for k in ks] + [((k, head_tn), BF16) for k in ks]
    bufs = list(stats_bufs) if stats else [] if out_buf is None else [out_buf]
    res = pl.pallas_call(
        functools.partial(kern, cast=True, n_alias=len(bufs), col_axis=0, tile_rows=head_tm),
        grid=(n_out // head_tn,),
        in_specs=in_specs + ins + [pl.BlockSpec(memory_space=pl.ANY)] * len(bufs),
        out_specs=outs, out_shape=shapes, input_output_aliases=aliases,
        compiler_params=_params(("arbitrary",),
                                _vmem_limit(blocks, single=[((head_tm, a.shape[1]), BF16) for a in acts],
                                            temps=[((head_tm, head_tn), F32)] * n_temps)),
        name=name + "_head",
    )(*acts, *[w for w, _, _ in weights], *extras, *side_in, *bufs)
    n_res = 3 if stats else 1
    prior, wbfs = res[:n_res], res[n_res:]

    r0 = head_tm // tm
    ins, outs, shapes, blocks = specs(tm, tn, lambda i, j: i + r0, lambda i, j: j)
    in_specs = [pl.BlockSpec((tm, a.shape[1]), lambda i, j: (i + r0, 0)) for a in acts]
    in_specs += [pl.BlockSpec(w_blk(k, tn), lambda i, j: w_idx(j)) for k in ks]
    n_in = len(in_specs) + len(ins)
    blocks += [((tm, a.shape[1]), BF16) for a in acts] + [((k, tn), BF16) for k in ks]
    if in_place:
        residual, unread = [prior[0]], list(prior[1:])
        aliases = {len(in_specs): 0}
        aliases.update({n_in + r: 1 + r for r in range(len(unread))})
    else:
        residual, unread = [], list(prior)
        aliases = {n_in: 0}
    res = pl.pallas_call(
        functools.partial(kern, cast=False, n_alias=len(unread), col_axis=1, tile_rows=tm),
        grid=(m // tm - r0, n_out // tn),
        in_specs=in_specs + ins + [pl.BlockSpec(memory_space=pl.ANY)] * len(unread),
        out_specs=outs, out_shape=shapes, input_output_aliases=aliases,
        compiler_params=_params(("parallel", "arbitrary"),
                                _vmem_limit(blocks, temps=[((tm, tn), F32)] * n_temps)),
        name=name + "_tail",
    )(*acts, *wbfs, *residual, *side_in, *unread)
    return tuple(res) if stats else res[0]


def _ffn_down(layer, act, w_down, h, next_gain, stats_bufs, *, tm=512, tn=512, head_tm=1024,
              head_tn=256, row_chunks=4):
    m, k = act.shape
    d = w_down.shape[2]
    kh = k // 2
    kern = functools.partial(_linear_kernel, w_rows_are_outputs=False, scaled=False, row_chunks=row_chunks)
    head_act = lambda part: pl.BlockSpec((head_tm, kh), lambda j: (0, part), pipeline_mode=pl.Buffered(1))
    head_w = lambda part: pl.BlockSpec((None, kh, head_tn), lambda j: (layer, part, j))
    tile = pl.BlockSpec((head_tm, head_tn), lambda j: (0, j))
    w_out = pl.BlockSpec((kh, head_tn), lambda j: (0, j))
    w_blocks = [((kh, head_tn), F32), ((kh, head_tn), BF16)]
    single = [((head_tm, kh), BF16)]
    temps = [((head_tm, head_tn), F32)] * 2

    partial, wbf_a = pl.pallas_call(
        functools.partial(kern, w_act=(0,), n_acts=1, n_extras=0, epilogue=lambda dd, e, rs: dd[0],
                          cast=True, stats=False, n_alias=0, col_axis=0, tile_rows=head_tm),
        grid=(d // head_tn,),
        in_specs=[head_act(0), head_w(0)],
        out_specs=[tile, w_out],
        out_shape=[jax.ShapeDtypeStruct((head_tm, d), F32), jax.ShapeDtypeStruct((kh, d), BF16)],
        compiler_params=_params(("arbitrary",),
                                _vmem_limit(w_blocks + [((head_tm, head_tn), F32)], single=single, temps=temps)),
        name="ffn_down_head_a",
    )(act, w_down)

    stat_specs = [tile, pl.BlockSpec((head_tm, 1), lambda j: (0, 0))]
    h_new, hb, ssq, wbf_b = pl.pallas_call(
        functools.partial(kern, w_act=(0,), n_acts=1, n_extras=2,
                          epilogue=lambda dd, e, rs: e[0] + 0.5 * (e[1] + dd[0]),
                          cast=True, stats=True, n_alias=2, col_axis=0, tile_rows=head_tm),
        grid=(d // head_tn,),
        in_specs=[head_act(1), head_w(1), tile, tile, pl.BlockSpec((1, head_tn), lambda j: (0, j)),
                  pl.BlockSpec(memory_space=pl.ANY), pl.BlockSpec(memory_space=pl.ANY)],
        out_specs=[tile] + stat_specs + [w_out],
        out_shape=[jax.ShapeDtypeStruct((m, d), F32), jax.ShapeDtypeStruct((m, d), BF16),
                   jax.ShapeDtypeStruct((m, 1), F32), jax.ShapeDtypeStruct((kh, d), BF16)],
        input_output_aliases={2: 0, 5: 1, 6: 2},
        compiler_params=_params(("arbitrary",), _vmem_limit(
            w_blocks + [((head_tm, head_tn), F32)] * 3 + [((head_tm, head_tn), BF16), ((head_tm, LANES), F32)],
            single=single, temps=temps)),
        name="ffn_down_head_b",
    )(act, w_down, h, partial, next_gain, *stats_bufs)

    r0 = head_tm // tm
    tile = pl.BlockSpec((tm, tn), lambda i, j: (i + r0, j))
    acts_t = [pl.BlockSpec((tm, kh), lambda i, j, part=part: (i + r0, part)) for part in range(2)]
    w_t = pl.BlockSpec((kh, tn), lambda i, j: (0, j))
    return pl.pallas_call(
        functools.partial(kern, w_act=(0, 1), n_acts=2, n_extras=1,
                          epilogue=lambda dd, e, rs: e[0] + 0.5 * (dd[0] + dd[1]),
                          cast=False, stats=True, n_alias=2, col_axis=1, tile_rows=tm),
        grid=(m // tm - r0, d // tn),
        in_specs=acts_t + [w_t, w_t, tile, pl.BlockSpec((1, tn), lambda i, j: (0, j)),
                           pl.BlockSpec(memory_space=pl.ANY), pl.BlockSpec(memory_space=pl.ANY)],
        out_specs=[tile, tile, pl.BlockSpec((tm, 1), lambda i, j: (i + r0, 0))],
        out_shape=[jax.ShapeDtypeStruct((m, d), F32), jax.ShapeDtypeStruct((m, d), BF16),
                   jax.ShapeDtypeStruct((m, 1), F32)],
        input_output_aliases={4: 0, 6: 1, 7: 2},
        compiler_params=_params(("parallel", "arbitrary"), _vmem_limit(
            [((tm, kh), BF16), ((kh, tn), BF16)] * 2 + [((tm, tn), F32)] * 2
            + [((tm, tn), BF16), ((tm, LANES), F32)], temps=[((tm, tn), F32)] * 2)),
        name="ffn_down_tail",
    )(act, act, wbf_a, wbf_b, h_new, next_gain, hb, ssq)


def _swiglu_ffn(layer, hb, ssq, w_gate, w_up, w_down, h, next_gain, act_buf):
    def gate_up(d, e, rs):
        g = d[0] * rs
        return g * jax.nn.sigmoid(g) * (d[1] * rs)

    act = _fused_linear("ffn_gateup", layer, [hb], [(w_gate, 0, 0), (w_up, 0, 0)], [], gate_up,
                        BF16, D_FF, tm=2048, tn=256, head_tm=2048, head_tn=256, n_temps=4,
                        row_ssq=ssq, out_buf=act_buf, row_chunks=8)
    return tuple(_ffn_down(layer, act, w_down, h, next_gain, (hb, ssq))) + (act,)


def _log_sigmoid(x):
    return jnp.minimum(x, 0.0) - jnp.log1p(jnp.exp(-jnp.abs(x)))


def _forget_kernel(u_ref, ssq_ref, w_ref, b_ref, c_ref, carry_ref, *, ts):
    @pl.when(pl.program_id(1) == 0)
    def _():
        carry_ref[...] = jnp.zeros_like(carry_ref)

    w = w_ref[...].astype(BF16)
    w = jnp.concatenate([w, jnp.zeros((LANES - N_HEADS, w.shape[1]), BF16)], axis=0)
    logit = lax.dot_general(u_ref[0], w, _NT, preferred_element_type=F32)
    logit = logit * _row_scale(ssq_ref[0]) + b_ref[...]
    c = _log_sigmoid(logit)
    row = lax.broadcasted_iota(jnp.int32, c.shape, 0)
    shift = 1
    while shift < ts:
        c = c + jnp.where(row >= shift, pltpu.roll(c, shift, axis=0), 0.0)
        shift *= 2
    c = c + carry_ref[...]
    c_ref[0] = c
    carry_ref[...] = c[ts - 1:ts, :]


def _forget_cumsum(layer, u3, ssq3, w_in_t, bias, ts=512):
    b, s, d = u3.shape
    blocks = [((1, ts, d), BF16), ((1, ts, LANES), F32), ((N_HEADS, d), F32), ((1, LANES), F32),
              ((1, ts, LANES), F32)]
    return pl.pallas_call(
        functools.partial(_forget_kernel, ts=ts),
        grid=(b, s // ts),
        in_specs=[pl.BlockSpec((1, ts, d), lambda bi, si: (bi, si, 0)),
                  pl.BlockSpec((1, ts, 1), lambda bi, si: (bi, si, 0)),
                  pl.BlockSpec((None, N_HEADS, d), lambda bi, si: (layer, N_QKV // N_HEADS, 0)),
                  pl.BlockSpec((1, LANES), lambda bi, si: (0, 0))],
        out_specs=pl.BlockSpec((1, ts, LANES), lambda bi, si: (bi, si, 0)),
        out_shape=jax.ShapeDtypeStruct((b, s, LANES), F32),
        scratch_shapes=[pltpu.VMEM((1, LANES), F32)],
        compiler_params=_params(("parallel", "arbitrary"),
                                _vmem_limit(blocks, temps=[((ts, LANES), F32)] * 8 + [((LANES, d), BF16)])),
        name="forget_cumsum",
    )(u3, ssq3, w_in_t, bias)


_HPS = 8
_HW = _HPS * HEAD_DIM


def _head_cols(hh):
    return slice(hh * HEAD_DIM, (hh + 1) * HEAD_DIM)


def _fox_kernel(q_ref, k_ref, v_ref, c_ref, ct_ref, o_ref, *, tq):
    hp = pl.program_id(1)
    i = pl.program_id(2)
    q_start = pl.multiple_of(i * tq, tq)
    half = tq // 2
    q = [(q_ref[0, :, _head_cols(hh)].astype(F32) * Q_SCALE).astype(BF16)
         for hh in range(_HPS)]
    c_t = [ct_ref[0, hh, :, pl.ds(q_start, tq)] * LOG2E for hh in range(_HPS)]

    def block(start, nk, q_lo, carry, hh, diag_offset=None):
        m, l, acc = carry
        nq = tq - q_lo
        k = k_ref[0, pl.ds(start, nk), _head_cols(hh)]
        v = v_ref[0, pl.ds(start, nk), _head_cols(hh)]
        lane = lax.broadcasted_iota(jnp.int32, (nk, LANES), 1)
        c_s = jnp.sum(jnp.where(lane == hp * _HPS + hh, c_ref[0, pl.ds(start, nk), :], 0.0),
                      axis=1, keepdims=True)
        x = lax.dot_general(k, q[hh][q_lo:, :], _NT, preferred_element_type=F32) - c_s * LOG2E
        if diag_offset is not None:
            key = lax.broadcasted_iota(jnp.int32, (nk, nq), 0) + diag_offset
            qry = lax.broadcasted_iota(jnp.int32, (nk, nq), 1) + q_lo
            x = jnp.where(key <= qry, x, -jnp.inf)
        ct = c_t[hh][:, q_lo:]
        m_new = jnp.maximum(m, jnp.max(x, axis=0, keepdims=True) + ct)
        p = jnp.exp2(x + (ct - m_new))
        alpha = jnp.exp2(m - m_new)
        l = alpha * l + jnp.sum(p, axis=0, keepdims=True)
        acc = alpha * acc + lax.dot_general(v, p.astype(BF16), _TN, preferred_element_type=F32)
        return m_new, l, acc

    def full_blocks(j, carries):
        start = pl.multiple_of(j * tq, tq)
        return tuple(block(start, tq, 0, carries[hh], hh) for hh in range(_HPS))

    init = (jnp.full((1, tq), -jnp.inf, F32), jnp.zeros((1, tq), F32),
            jnp.zeros((HEAD_DIM, tq), F32))
    carries = lax.fori_loop(0, i, full_blocks, (init,) * _HPS)
    for hh in range(_HPS):
        carry = block(q_start, half, 0, carries[hh], hh, diag_offset=0)
        late = block(pl.multiple_of(q_start + half, half), half, half,
                     tuple(t[:, half:] for t in carry), hh, diag_offset=half)
        _, l, acc = (jnp.concatenate([t[:, :half], u], axis=1) for t, u in zip(carry, late))
        o_ref[0, :, _head_cols(hh)] = (acc / l).T.astype(o_ref.dtype)


def _fox_attention(z3, c, ct, tq=1024):
    b, s, _ = z3.shape
    npair = N_HEADS // _HPS
    blocks = [((1, tq, _HW), BF16), ((1, s, _HW), BF16), ((1, s, _HW), BF16),
              ((1, s, LANES), F32), ((1, _HPS, 1, s), F32), ((1, tq, _HW), BF16)]
    return pl.pallas_call(
        functools.partial(_fox_kernel, tq=tq),
        grid=(b, npair, s // tq),
        in_specs=[pl.BlockSpec((1, tq, _HW), lambda bi, hp, i: (bi, i, hp)),
                  pl.BlockSpec((1, s, _HW), lambda bi, hp, i: (bi, 0, npair + hp)),
                  pl.BlockSpec((1, s, _HW), lambda bi, hp, i: (bi, 0, 2 * npair + hp)),
                  pl.BlockSpec((1, s, LANES), lambda bi, hp, i: (bi, 0, 0)),
                  pl.BlockSpec((1, _HPS, 1, s), lambda bi, hp, i: (bi, hp, 0, 0))],
        out_specs=pl.BlockSpec((1, tq, _HW), lambda bi, hp, i: (bi, i, hp)),
        out_shape=jax.ShapeDtypeStruct((b, s, WIDTH), BF16),
        compiler_params=_params(("parallel", "parallel", "arbitrary"),
                                _vmem_limit(blocks, temps=[((tq, tq), F32)] * 6 * _HPS)),
        name="fox_attention",
    )(z3, z3, z3, c, ct)


_QB = 8 * CHUNK
_N_BIAS_VEC = 12


def _bias_pieces_index():
    e = 128 * (np.arange(_N_BIAS_VEC)[:, None] - 4) + np.arange(LANES)[None, :]
    dist = _QB - e
    return np.clip(dist, -REL_CLIP, REL_CLIP) + REL_CLIP


def _chunk_kernel(q_ref, kp_ref, kc_ref, vp_ref, vc_ref, g_ref, o_ref, bias_ref):
    bi = pl.program_id(1)
    i = pl.program_id(2)

    @pl.when((bi == 0) & (i == 0))
    def _():
        r = lax.broadcasted_iota(jnp.int32, (LANES, LANES), 0)
        col = lax.broadcasted_iota(jnp.int32, (LANES, LANES), 1)
        upper = col >= r
        for hh in range(_HPS):
            rolled = [pltpu.roll(jnp.broadcast_to(g_ref[hh, k:k + 1, :] * LOG2E, (LANES, LANES)),
                                 0, 1, stride=1, stride_axis=0) for k in range(_N_BIAS_VEC)]
            for rb in range(_QB // LANES):
                for cb in range(2 * _QB // LANES):
                    delta = cb - rb + 4
                    tile = jnp.where(upper, rolled[delta], rolled[delta - 1])
                    q_chunk = (rb * LANES + r) // CHUNK
                    k_chunk = (cb * LANES + col) // CHUNK
                    ok = (k_chunk >= q_chunk) & (k_chunk <= q_chunk + LEFT_CHUNKS)
                    bias_ref[hh, cb * LANES:(cb + 1) * LANES, rb * LANES:(rb + 1) * LANES] = (
                        jnp.where(ok, tile, -jnp.inf).T)

    def attend(with_prev):
        for hh in range(_HPS):
            cols = _head_cols(hh)
            q = (q_ref[0, :, cols].astype(F32) * Q_SCALE).astype(BF16)
            x_cur = (lax.dot_general(kc_ref[0, :, cols], q, _NT, preferred_element_type=F32)
                     + bias_ref[hh, _QB:, :])
            m = jnp.max(x_cur, axis=0, keepdims=True)
            if with_prev:
                x_prev = (lax.dot_general(kp_ref[0, :, cols], q, _NT, preferred_element_type=F32)
                          + bias_ref[hh, :_QB, :])
                m = jnp.maximum(m, jnp.max(x_prev, axis=0, keepdims=True))
            p_cur = jnp.exp2(x_cur - m)
            l = jnp.sum(p_cur, axis=0, keepdims=True)
            acc = lax.dot_general(vc_ref[0, :, cols], p_cur.astype(BF16), _TN,
                                  preferred_element_type=F32)
            if with_prev:
                p_prev = jnp.exp2(x_prev - m)
                l = l + jnp.sum(p_prev, axis=0, keepdims=True)
                acc = acc + lax.dot_general(vp_ref[0, :, cols], p_prev.astype(BF16), _TN,
                                            preferred_element_type=F32)
            o_ref[0, :, cols] = (acc / l).T.astype(o_ref.dtype)

    @pl.when(i == 0)
    def _():
        attend(False)

    @pl.when(i > 0)
    def _():
        attend(True)


def _chunk_attention(z3, bias_pieces):
    b, s, _ = z3.shape
    nb = s // _QB
    npair = N_HEADS // _HPS
    qo, ko, vo = 3 * npair, 4 * npair, 5 * npair
    blk = (1, _QB, _HW)
    blocks = [(blk, BF16)] * 6 + [((_HPS, 16, LANES), F32)]
    prev = lambda i: jnp.maximum(i - 1, 0)
    return pl.pallas_call(
        _chunk_kernel,
        grid=(npair, b, nb),
        in_specs=[pl.BlockSpec(blk, lambda hp, bi, i: (bi, i, qo + hp)),
                  pl.BlockSpec(blk, lambda hp, bi, i: (bi, prev(i), ko + hp)),
                  pl.BlockSpec(blk, lambda hp, bi, i: (bi, i, ko + hp)),
                  pl.BlockSpec(blk, lambda hp, bi, i: (bi, prev(i), vo + hp)),
                  pl.BlockSpec(blk, lambda hp, bi, i: (bi, i, vo + hp)),
                  pl.BlockSpec((_HPS, 16, LANES), lambda hp, bi, i: (hp, 0, 0))],
        out_specs=pl.BlockSpec(blk, lambda hp, bi, i: (bi, i, hp)),
        out_shape=jax.ShapeDtypeStruct((b, s, WIDTH), BF16),
        scratch_shapes=[pltpu.VMEM((_HPS, 2 * _QB, _QB), F32)],
        compiler_params=_params(("arbitrary", "arbitrary", "arbitrary"),
                                _vmem_limit(blocks, single=[((_HPS, 2 * _QB, _QB), F32)],
                                            temps=[((_QB, _QB), F32)] * 8 * _HPS)),
        name="chunk_attention",
    )(z3, z3, z3, z3, z3, bias_pieces)


def kernel(x, p, ffn1_norm, ffn1_w_gate, ffn1_w_up, ffn1_w_down, mix_norm, w_in, fox_forget_bias,
           rel_bias, w_branch_gate, w_proj_a, w_proj_b, w_out, ffn2_norm, ffn2_w_gate, ffn2_w_up,
           ffn2_w_down, ple_norm, ple_w_gate, ple_w_proj, final_norm):
    forget_bias = jnp.pad(fox_forget_bias, ((0, 0), (0, LANES - N_HEADS)))[:, None, :]
    p_bf = p.astype(BF16).reshape(DEPTH, M_TOK, PLE_DIM)
    bias_pieces = jnp.pad(rel_bias[:, :, _bias_pieces_index()],
                          ((0, 0), (0, 0), (0, 16 - _N_BIAS_VEC), (0, 0)))
    sig = jax.nn.sigmoid
    gain = lambda g: g.reshape(1, D_MODEL)
    w_in_t = jnp.swapaxes(w_in, 1, 2)

    h, hb, ssq, ssq_spare = _prenorm(x.reshape(M_TOK, D_MODEL), ffn1_norm[0])
    act_buf = z_buf = mix_buf = None
    for i in range(DEPTH):
        h, hb, ssq, act_buf = _swiglu_ffn(i, hb, ssq, ffn1_w_gate, ffn1_w_up, ffn1_w_down, h,
                                          gain(mix_norm[i]), act_buf)

        z = _fused_linear("qkv_proj", i, [hb], [(w_in_t, 0, 0)], [], lambda d, e, rs: d[0] * rs,
                          BF16, N_QKV, tm=1024, tn=1024, head_tm=1024, head_tn=512, n_temps=2,
                          row_ssq=ssq, out_buf=z_buf, w_rows_are_outputs=True, row_chunks=4)
        z3, z_buf = z.reshape(BATCH, SEQ, N_QKV), z
        c = _forget_cumsum(i, hb.reshape(BATCH, SEQ, D_MODEL), ssq.reshape(BATCH, SEQ, 1), w_in_t,
                           forget_bias[i])
        ct = c[:, :, :N_HEADS].transpose(0, 2, 1)[:, :, None, :]
        attn_a = _fox_attention(z3, c, ct).reshape(M_TOK, WIDTH)
        attn_b = _chunk_attention(z3, bias_pieces[i]).reshape(M_TOK, WIDTH)
        mix = _fused_linear(
            "branch_mix", i, [hb, attn_a, attn_b],
            [(w_branch_gate, 0, 0), (w_branch_gate, 0, D_MODEL), (w_proj_a, 1, 0), (w_proj_b, 2, 0)],
            [], lambda d, e, rs: sig(d[0] * rs) * d[2] + sig(d[1] * rs) * d[3], BF16, D_MODEL,
            tm=512, tn=512, head_tm=512, head_tn=256, n_temps=6, row_ssq=ssq, out_buf=mix_buf,
            row_chunks=2)
        h, hb, ssq = _fused_linear("out_proj", i, [mix], [(w_out, 0, 0)], [h],
                                   lambda d, e, rs: e[0] + d[0], F32, D_MODEL,
                                   tm=1024, tn=512, head_tm=1024, head_tn=512, n_temps=2,
                                   next_gain=gain(ffn2_norm[i]), stats_bufs=(hb, ssq), row_chunks=4)

        h, hb, ssq, act_buf = _swiglu_ffn(i, hb, ssq, ffn2_w_gate, ffn2_w_up, ffn2_w_down, h,
                                          gain(ple_norm[i]), act_buf)

        last = i + 1 == DEPTH
        res = _fused_linear("ple", i, [hb, p_bf[i]], [(ple_w_gate, 0, 0), (ple_w_proj, 1, 0)], [h],
                            lambda d, e, rs: e[0] + sig(d[0] * rs) * d[1], F32, D_MODEL,
                            tm=1024, tn=512, head_tm=1024, head_tn=512, n_temps=4, row_ssq=ssq,
                            next_gain=None if last else gain(ffn1_norm[i + 1]),
                            stats_bufs=None if last else (mix, ssq_spare), row_chunks=4)
        if not last:
            mix_buf, ssq_spare = hb, ssq
        h, hb, ssq = (res, None, None) if last else res
    out = _rmsnorm(h, final_norm, F32)
    return out.reshape(BATCH, SEQ, D_MODEL)
```
